```python
import math
import jax, jax.numpy as jnp
from jax import lax
import numpy as np

D_MODEL = 1024
BATCH = 2
SEQ = 8192
DEPTH = 4
DEC_BATCH = 16
DEC_SEQ = 2048
PAST_LEN = 128

HEAD_DIM = 64
A_PATTERNS = ((128, 1), (512, 4), (2048, 16))
A_GROUPS = 3
A_HEADS = 8
A_WIDTH = A_HEADS * HEAD_DIM
B_HEADS = 8
B_KV_HEADS = 2
B_HALF = 128
B_WIDTH = B_HEADS * HEAD_DIM
C_HEADS = 4
C_VDIM = 2 * HEAD_DIM
C_WIDTH = C_HEADS * C_VDIM
N_BRANCH = 3
Q_BLOCK = 128
RMS_EPS = 1e-6
NEG_INF = -1e30
IN_SIZES = (
    A_GROUPS * A_WIDTH, A_GROUPS * A_WIDTH, A_GROUPS * A_WIDTH, A_WIDTH,
    B_WIDTH, B_KV_HEADS * HEAD_DIM, B_KV_HEADS * HEAD_DIM, B_WIDTH,
    2 * C_HEADS * HEAD_DIM, 2 * C_HEADS * HEAD_DIM, C_WIDTH, C_WIDTH,
    N_BRANCH * D_MODEL,
)
D_IN = sum(IN_SIZES)

kernel_name = 'hybrid_dilated_window_diff_encoder'


def rmsnorm(x, g):
    xf = x.astype(jnp.float32)
    y = xf * lax.rsqrt(jnp.mean(xf * xf, axis=-1, keepdims=True) + RMS_EPS)
    return (y * g.astype(jnp.float32)).astype(x.dtype)


def alibi_slopes(n):
    return jnp.asarray([2.0 ** (-8.0 * (i + 1) / n) for i in range(n)], dtype=jnp.float32)


def banded_attention(q, k, v, half, slopes, dist_scale, sink=None):
    n, s_len, h, d = q.shape
    hkv = k.shape[2]
    grp = h // hkv
    blk = half
    nb = -(-s_len // blk)
    pad = nb * blk - s_len
    f32 = jnp.float32
    qp = jnp.pad(q.astype(f32), ((0, 0), (0, pad), (0, 0), (0, 0)))
    kv_pad = ((0, 0), (blk, blk + pad), (0, 0), (0, 0))
    kb = jnp.pad(k.astype(f32), kv_pad).reshape(n, nb + 2, blk, hkv, k.shape[-1])
    vb = jnp.pad(v.astype(f32), kv_pad).reshape(n, nb + 2, blk, hkv, v.shape[-1])
    kw = jnp.concatenate([kb[:, :-2], kb[:, 1:-1], kb[:, 2:]], axis=2)
    vw = jnp.concatenate([vb[:, :-2], vb[:, 1:-1], vb[:, 2:]], axis=2)
    qb = qp.reshape(n, nb, blk, hkv, grp, d)
    s = jnp.einsum('nbqhgd,nbkhd->nbhgqk', qb, kw) * (d ** -0.5)
    koff = jnp.arange(3 * blk) - blk
    rel = koff[None, :] - jnp.arange(blk)[:, None]
    kpos = jnp.arange(nb)[:, None] * blk + koff[None, :]
    mask = (jnp.abs(rel) <= half)[None] & ((kpos >= 0) & (kpos < s_len))[:, None, :]
    bias = -slopes.astype(f32).reshape(hkv, grp, 1, 1) * (jnp.abs(rel).astype(f32) * dist_scale)
    s = jnp.where(mask[None, :, None, None], s + bias, NEG_INF)
    m = jnp.max(s, axis=-1)
    if sink is not None:
        sk = sink.astype(f32).reshape(1, 1, hkv, grp, 1)
        m = jnp.maximum(m, sk)
    e = jnp.exp(s - m[..., None])
    l = jnp.sum(e, axis=-1)
    if sink is not None:
        l = l + jnp.exp(sk - m)
    o = jnp.einsum('nbhgqk,nbkhd->nbqhgd', e, vw) / jnp.moveaxis(l, -1, 2)[..., None]
    o = o.reshape(n, nb * blk, h, v.shape[-1])[:, :s_len]
    lse = jnp.moveaxis(m + jnp.log(l), -1, 2).reshape(n, nb * blk, h)[:, :s_len]
    return o, lse


def dilated_attention(q, k, v):
    b, s_len = q.shape[:2]
    slopes_all = alibi_slopes(A_GROUPS * A_HEADS).reshape(A_GROUPS, A_HEADS)
    outs, lses = [], []
    for gi, (window, dil) in enumerate(A_PATTERNS):
        half = window // (2 * dil)
        sub = s_len // dil

        def to_sub(t):
            t = t[:, :, gi].reshape(b, sub, dil, A_HEADS, t.shape[-1])
            return jnp.swapaxes(t, 1, 2).reshape(b * dil, sub, A_HEADS, t.shape[-1])

        o, lse = banded_attention(to_sub(q), to_sub(k), to_sub(v), half, slopes_all[gi], float(dil))
        o = jnp.swapaxes(o.reshape(b, dil, sub, A_HEADS, HEAD_DIM), 1, 2).reshape(b, s_len, A_HEADS, HEAD_DIM)
        lse = jnp.swapaxes(lse.reshape(b, dil, sub, A_HEADS), 1, 2).reshape(b, s_len, A_HEADS)
        outs.append(o)
        lses.append(lse)
    w = jax.nn.softmax(jnp.stack(lses, 0), axis=0)
    return jnp.einsum('gbsh,gbshd->bshd', w, jnp.stack(outs, 0))


def differential_attention(q, k, v, lam, lam_init, subln_g):
    f32 = jnp.float32
    b, s_len = q.shape[:2]
    nb = s_len // Q_BLOCK
    slopes = alibi_slopes(C_HEADS)
    kpos = jnp.arange(s_len)
    kf = k.astype(f32)
    vf = v.astype(f32)
    qb = jnp.moveaxis(q.astype(f32).reshape(b, nb, Q_BLOCK, C_HEADS, 2, HEAD_DIM), 1, 0)

    def one_block(args):
        qblk, i = args
        qpos = i * Q_BLOCK + jnp.arange(Q_BLOCK)
        bias = -slopes[:, None, None] * jnp.abs(qpos[:, None] - kpos[None, :]).astype(f32)
        s = jnp.einsum('bqhcd,bkhcd->bchqk', qblk, kf) * (HEAD_DIM ** -0.5) + bias
        p = jax.nn.softmax(s, axis=-1)
        a = p[:, 0] - lam * p[:, 1]
        return jnp.einsum('bhqk,bkhd->bqhd', a, vf)

    o = lax.map(one_block, (qb, jnp.arange(nb)))
    o = jnp.moveaxis(o, 0, 1).reshape(b, s_len, C_HEADS, C_VDIM)
    return rmsnorm(o, subln_g) * (1.0 - lam_init)


def hybrid_layer(x, layer_idx, norm_g, w_in, w_oa, w_ob, w_oc, w_out, b_sink,
                 lam_q1, lam_k1, lam_q2, lam_k2, c_subln_g):
    b, s_len, _ = x.shape
    h = rmsnorm(x, norm_g)
    proj = h @ w_in
    splits = [int(c) for c in np.cumsum(IN_SIZES)[:-1]]
    qa, ka, va, ga, qb, kb, vb, gb, qc, kc, vc, gc, gm = jnp.split(proj, splits, axis=-1)
    a5 = lambda t: t.reshape(b, s_len, A_GROUPS, A_HEADS, HEAD_DIM)
    oa = dilated_attention(a5(qa), a5(ka), a5(va)).reshape(b, s_len, A_WIDTH).astype(x.dtype)
    ya = (oa * jax.nn.silu(ga)) @ w_oa
    ob, _ = banded_attention(qb.reshape(b, s_len, B_HEADS, HEAD_DIM),
                             kb.reshape(b, s_len, B_KV_HEADS, HEAD_DIM),
                             vb.reshape(b, s_len, B_KV_HEADS, HEAD_DIM),
                             B_HALF, alibi_slopes(B_HEADS), 1.0, sink=b_sink)
    ob = ob.reshape(b, s_len, B_WIDTH).astype(x.dtype)
    yb = (ob * jax.nn.silu(gb)) @ w_ob
    lam_init = 0.8 - 0.6 * math.exp(-0.3 * layer_idx)
    f32 = jnp.float32
    lam = (jnp.exp(jnp.sum(lam_q1.astype(f32) * lam_k1.astype(f32)))
           - jnp.exp(jnp.sum(lam_q2.astype(f32) * lam_k2.astype(f32))) + lam_init)
    oc = differential_attention(qc.reshape(b, s_len, C_HEADS, 2, HEAD_DIM),
                                kc.reshape(b, s_len, C_HEADS, 2, HEAD_DIM),
                                vc.reshape(b, s_len, C_HEADS, C_VDIM), lam, lam_init, c_subln_g)
    oc = oc.reshape(b, s_len, C_WIDTH).astype(x.dtype)
    yc = (oc * jax.nn.silu(gc)) @ w_oc
    gates = jax.nn.sigmoid(gm.reshape(b, s_len, N_BRANCH, D_MODEL))
    mixed = gates[:, :, 0] * ya + gates[:, :, 1] * yb + gates[:, :, 2] * yc
    return x + mixed @ w_out


def setup_inputs(seed: int = 0) -> dict:
    key = jax.random.key(seed)
    ks = jax.random.split(key, 16)
    f32 = jnp.float32

    def nrm(k, shape, scale):
        return jax.random.normal(k, shape, f32) * scale

    return {
        'x_prompt': nrm(ks[0], (BATCH, SEQ, D_MODEL), 1.0),
        'x_sample': nrm(ks[1], (DEC_BATCH, DEC_SEQ, D_MODEL), 1.0),
        'norm_g': 1.0 + nrm(ks[2], (DEPTH, D_MODEL), 0.02),
        'w_in': nrm(ks[3], (DEPTH, D_MODEL, D_IN), D_MODEL ** -0.5),
        'w_oa': nrm(ks[4], (DEPTH, A_WIDTH, D_MODEL), A_WIDTH ** -0.5),
        'w_ob': nrm(ks[5], (DEPTH, B_WIDTH, D_MODEL), B_WIDTH ** -0.5),
        'w_oc': nrm(ks[6], (DEPTH, C_WIDTH, D_MODEL), C_WIDTH ** -0.5),
        'w_out': nrm(ks[7], (DEPTH, D_MODEL, D_MODEL), D_MODEL ** -0.5),
        'b_sink': nrm(ks[8], (DEPTH, B_HEADS), 0.5),
        'lam_q1': nrm(ks[9], (DEPTH, HEAD_DIM), 0.1),
        'lam_k1': nrm(ks[10], (DEPTH, HEAD_DIM), 0.1),
        'lam_q2': nrm(ks[11], (DEPTH, HEAD_DIM), 0.1),
        'lam_k2': nrm(ks[12], (DEPTH, HEAD_DIM), 0.1),
        'c_subln_g': 1.0 + nrm(ks[13], (DEPTH, C_VDIM), 0.02),
        'final_norm_g': 1.0 + nrm(ks[14], (D_MODEL,), 0.02),
    }


def reference(x_prompt, x_sample, norm_g, w_in, w_oa, w_ob, w_oc, w_out, b_sink,
              lam_q1, lam_k1, lam_q2, lam_k2, c_subln_g, final_norm_g):
    def trunk(x):
        for l in range(DEPTH):
            x = hybrid_layer(x, l, norm_g[l], w_in[l], w_oa[l], w_ob[l], w_oc[l], w_out[l], b_sink[l],
                             lam_q1[l], lam_k1[l], lam_q2[l], lam_k2[l], c_subln_g[l])
        return rmsnorm(x, final_norm_g)

    y_prompt = trunk(x_prompt)
    y_sample = trunk(x_sample)
    return (y_prompt, y_sample)
```

```python
import functools
import math

import numpy as np
import jax
import jax.numpy as jnp
from jax import lax
from jax.experimental import pallas as pl
from jax.experimental.pallas import tpu as pltpu

F32 = jnp.float32
BF16 = jnp.bfloat16

D_MODEL = 1024
DEPTH = 4
HEAD_DIM = 64
A_PATTERNS = ((128, 1), (512, 4), (2048, 16))
A_GROUPS = 3
A_HEADS = 8
A_WIDTH = A_HEADS * HEAD_DIM
A_HALF = 64
B_HEADS = 8
B_KV_HEADS = 2
B_HALF = 128
B_WIDTH = B_HEADS * HEAD_DIM
C_HEADS = 4
C_VDIM = 2 * HEAD_DIM
C_WIDTH = C_HEADS * C_VDIM
RMS_EPS = 1e-6
NEG_INF = -1e30
IN_SIZES = (
    A_GROUPS * A_WIDTH, A_GROUPS * A_WIDTH, A_GROUPS * A_WIDTH, A_WIDTH,
    B_WIDTH, B_KV_HEADS * HEAD_DIM, B_KV_HEADS * HEAD_DIM, B_WIDTH,
    2 * C_HEADS * HEAD_DIM, 2 * C_HEADS * HEAD_DIM, C_WIDTH, C_WIDTH,
    3 * D_MODEL,
)
IN_OFFSETS = tuple(int(c) for c in np.cumsum((0,) + IN_SIZES))

PROJ_TM = 512
POST_TM = 256
A_TQ = 128
A_STEP = 512
B_TQ = 256
C_TQ = 256
C_TK = 512
VMEM_LIMIT = 56 * 2**20

N_FEAT = B_WIDTH + B_KV_HEADS * HEAD_DIM + 2 * C_HEADS * HEAD_DIM + C_WIDTH


def _alibi_slopes(n):
    return np.asarray([2.0 ** (-8.0 * (i + 1) / n) for i in range(n)], dtype=np.float32)


def _rms(x, g):
    ms = jnp.mean(x * x, axis=-1, keepdims=True)
    return x * lax.rsqrt(ms + RMS_EPS) * g


def _const_spec(shape):
    nd = len(shape)
    return pl.BlockSpec(shape, lambda *_: (0,) * nd, pipeline_mode=pl.Buffered(1))


def _proj_kernel(x_ref, g_ref, wqa_ref, wka_ref, wva_ref, wkb_ref, wkc_ref, wft_ref,
                 qa_ref, ka_ref, va_ref, kb_ref, kc_ref, qbt_ref, vbt_ref, qct_ref, vct_ref):
    h = _rms(x_ref[...], g_ref[...]).astype(BF16)
    for w_ref, o_ref in ((wqa_ref, qa_ref), (wka_ref, ka_ref), (wva_ref, va_ref),
                         (wkb_ref, kb_ref), (wkc_ref, kc_ref)):
        o_ref[...] = jnp.dot(h, w_ref[...], preferred_element_type=F32).astype(BF16)
    ft = lax.dot_general(wft_ref[...], h, (((1,), (1,)), ((), ())),
                         preferred_element_type=F32).astype(BF16)
    r0 = 0
    for o_ref, rows in ((qbt_ref, B_WIDTH), (vbt_ref, B_KV_HEADS * HEAD_DIM),
                        (qct_ref, 2 * C_HEADS * HEAD_DIM), (vct_ref, C_WIDTH)):
        nblk, _, width = o_ref.shape
        for c in range(nblk):
            o_ref[c] = ft[r0:r0 + rows, c * width:(c + 1) * width]
        r0 += rows


def _proj_call(x, g, w):
    t = x.shape[0]
    tm = PROJ_TM
    tok = lambda n: pl.BlockSpec((tm, n), lambda i: (i, 0))
    feat = lambda rows, width: pl.BlockSpec((tm // width, rows, width), lambda i: (i, 0, 0))
    a_cols = A_GROUPS * A_WIDTH
    out_shape = (
        jax.ShapeDtypeStruct((t, a_cols), BF16), jax.ShapeDtypeStruct((t, a_cols), BF16),
        jax.ShapeDtypeStruct((t, a_cols), BF16),
        jax.ShapeDtypeStruct((t, B_KV_HEADS * HEAD_DIM), BF16),
        jax.ShapeDtypeStruct((t, 2 * C_HEADS * HEAD_DIM), BF16),
        jax.ShapeDtypeStruct((t // B_TQ, B_WIDTH, B_TQ), BF16),
        jax.ShapeDtypeStruct((t // B_HALF, B_KV_HEADS * HEAD_DIM, B_HALF), BF16),
        jax.ShapeDtypeStruct((t // C_TQ, 2 * C_HEADS * HEAD_DIM, C_TQ), BF16),
        jax.ShapeDtypeStruct((t // C_TK, C_WIDTH, C_TK), BF16),
    )
    return pl.pallas_call(
        _proj_kernel,
        grid=(t // tm,),
        in_specs=[tok(D_MODEL), _const_spec((1, D_MODEL)),
                  _const_spec(w["qa"].shape), _const_spec(w["ka"].shape), _const_spec(w["va"].shape),
                  _const_spec(w["kb"].shape), _const_spec(w["kc"].shape), _const_spec(w["ft"].shape)],
        out_specs=(tok(a_cols), tok(a_cols), tok(a_cols), tok(B_KV_HEADS * HEAD_DIM),
                   tok(2 * C_HEADS * HEAD_DIM),
                   feat(B_WIDTH, B_TQ), feat(B_KV_HEADS * HEAD_DIM, B_HALF),
                   feat(2 * C_HEADS * HEAD_DIM, C_TQ), feat(C_WIDTH, C_TK)),
        out_shape=out_shape,
        compiler_params=pltpu.CompilerParams(dimension_semantics=("parallel",),
                                             vmem_limit_bytes=VMEM_LIMIT),
        name="proj",
    )(x, g, w["qa"], w["ka"], w["va"], w["kb"], w["kc"], w["ft"])


def _a_bias(gi):
    dil = A_PATTERNS[gi][1]
    slopes = _alibi_slopes(A_GROUPS * A_HEADS).reshape(A_GROUPS, A_HEADS)[gi]
    rel = (np.arange(A_TQ + 2 * A_HALF)[None, :] - A_HALF) - np.arange(A_TQ)[:, None]
    dist = np.abs(rel).astype(np.float32) * np.float32(dil)
    bias = -slopes[:, None, None] * dist[None]
    return np.where(np.abs(rel)[None] <= A_HALF, bias, np.float32(NEG_INF)).astype(np.float32)


def _a_kernel(q_ref, kp_ref, kc_ref, kn_ref, vp_ref, vc_ref, vn_ref, bias_ref,
              o_ref, lse_ref, kfull, vfull, *, step, n_steps):
    ui = pl.program_id(2)
    tkw = A_TQ + 2 * A_HALF
    kfull[0:A_HALF] = kp_ref[...]
    kfull[A_HALF:A_HALF + step] = kc_ref[...]
    kfull[A_HALF + step:] = kn_ref[...]
    vfull[0:A_HALF] = vp_ref[...]
    vfull[A_HALF:A_HALF + step] = vc_ref[...]
    vfull[A_HALF + step:] = vn_ref[...]
    n_sb = step // A_TQ
    col = lax.broadcasted_iota(jnp.int32, (A_TQ, tkw), 1)
    lane = lax.broadcasted_iota(jnp.int32, (tkw, 2 * HEAD_DIM), 1)
    lane_o = lax.broadcasted_iota(jnp.int32, (A_TQ, 2 * HEAD_DIM), 1)
    for sb in range(n_sb):
        pen = None
        if sb == 0:
            pen = jnp.where((col < A_HALF) & (ui == 0), NEG_INF, 0.0)
        if sb == n_sb - 1:
            pen_hi = jnp.where((col >= A_TQ + A_HALF) & (ui == n_steps - 1), NEG_INF, 0.0)
            pen = pen_hi if pen is None else pen + pen_hi
        r0 = sb * A_TQ
        for p in range(A_HEADS // 2):
            c0 = p * 2 * HEAD_DIM
            qp = q_ref[r0:r0 + A_TQ, c0:c0 + 2 * HEAD_DIM]
            kw = kfull[r0:r0 + tkw, c0:c0 + 2 * HEAD_DIM]
            vw = vfull[r0:r0 + tkw, c0:c0 + 2 * HEAD_DIM]
            o_pair = None
            lse_pair = None
            for hh in range(2):
                in_half = (lane >= hh * HEAD_DIM) & (lane < (hh + 1) * HEAD_DIM)
                kz = jnp.where(in_half, kw, jnp.zeros_like(kw))
                vz = jnp.where(in_half, vw, jnp.zeros_like(vw))
                s = lax.dot_general(qp, kz, (((1,), (1,)), ((), ())), preferred_element_type=F32)
                s = s + bias_ref[2 * p + hh]
                if pen is not None:
                    s = s + pen
                m = jnp.max(s, axis=1, keepdims=True)
                e = jnp.exp(s - m)
                l = jnp.sum(e, axis=1, keepdims=True)
                o = jnp.dot(e.astype(BF16), vz, preferred_element_type=F32) / l
                lse = jnp.broadcast_to(m + jnp.log(l), (A_TQ, 2 * HEAD_DIM))
                if hh == 0:
                    o_pair, lse_pair = o, lse
                else:
                    o_pair = o_pair + o
                    lse_pair = jnp.where(lane_o < HEAD_DIM, lse_pair, lse)
            o_ref[r0:r0 + A_TQ, c0:c0 + 2 * HEAD_DIM] = o_pair
            lse_ref[r0:r0 + A_TQ, c0:c0 + 2 * HEAD_DIM] = lse_pair


def _a_call(qa, ka, va, gi, batch, seq):
    dil = A_PATTERNS[gi][1]
    sub = seq // dil
    step = min(A_STEP, sub)
    n_steps = sub // step
    t = batch * seq
    cols = A_GROUPS * A_WIDTH
    view = lambda a, c: a.reshape(t // dil, dil * c)
    hb = step // A_HALF
    nhb = sub // A_HALF
    cur = pl.BlockSpec((step, A_WIDTH), lambda b, r, u: (b * n_steps + u, r * A_GROUPS + gi))
    prev = pl.BlockSpec((A_HALF, A_WIDTH),
                        lambda b, r, u: (b * nhb + jnp.maximum(u * hb - 1, 0), r * A_GROUPS + gi))
    nxt = pl.BlockSpec((A_HALF, A_WIDTH),
                       lambda b, r, u: (b * nhb + jnp.minimum((u + 1) * hb, nhb - 1), r * A_GROUPS + gi))
    out = pl.BlockSpec((step, A_WIDTH), lambda b, r, u: (b * n_steps + u, r))
    bias = jnp.asarray(_a_bias(gi))
    o, lse = pl.pallas_call(
        functools.partial(_a_kernel, step=step, n_steps=n_steps),
        grid=(batch, dil, n_steps),
        in_specs=[cur, prev, cur, nxt, prev, cur, nxt, _const_spec(bias.shape)],
        out_specs=(out, out),
        out_shape=(jax.ShapeDtypeStruct((t // dil, dil * A_WIDTH), F32),
                   jax.ShapeDtypeStruct((t // dil, dil * A_WIDTH), F32)),
        scratch_shapes=[pltpu.VMEM((step + 2 * A_HALF, A_WIDTH), BF16),
                        pltpu.VMEM((step + 2 * A_HALF, A_WIDTH), BF16)],
        compiler_params=pltpu.CompilerParams(
            dimension_semantics=("parallel", "parallel", "parallel"), vmem_limit_bytes=VMEM_LIMIT),
        name=f"mixer_a{gi}",
    )(view(qa, cols), view(ka, cols), view(ka, cols), view(ka, cols),
      view(va, cols), view(va, cols), view(va, cols), bias)
    return o.reshape(t, A_WIDTH), lse.reshape(t, A_WIDTH)


def _b_bias():
    slopes = _alibi_slopes(B_HEADS)
    rel = (np.arange(B_TQ + 2 * B_HALF)[:, None] - B_HALF) - np.arange(B_TQ)[None, :]
    bias = -slopes[:, None, None] * np.abs(rel).astype(np.float32)[None]
    return np.where(np.abs(rel)[None] <= B_HALF, bias, np.float32(NEG_INF)).astype(np.float32)


def _b_kernel(sink_ref, q_ref, kp_ref, kc_ref, kn_ref, vp_ref, vc_ref, vn_ref, bias_ref,
              o_ref, kfull, vfull, *, n_q):
    i = pl.program_id(1)
    tk = B_TQ + 2 * B_HALF
    kfull[0:B_HALF] = kp_ref[...]
    kfull[B_HALF:B_HALF + B_TQ] = kc_ref[...]
    kfull[B_HALF + B_TQ:] = kn_ref[...]
    vfull[:, 0:B_HALF] = vp_ref[0]
    for c in range(B_TQ // B_HALF):
        vfull[:, (c + 1) * B_HALF:(c + 2) * B_HALF] = vc_ref[c]
    vfull[:, B_HALF + B_TQ:] = vn_ref[0]
    row = lax.broadcasted_iota(jnp.int32, (tk, B_TQ), 0)
    pen = (jnp.where((row < B_HALF) & (i == 0), NEG_INF, 0.0)
           + jnp.where((row >= B_HALF + B_TQ) & (i == n_q - 1), NEG_INF, 0.0))
    k = kfull[...]
    zeros = jnp.zeros((HEAD_DIM, B_TQ), BF16)
    grp = B_HEADS // B_KV_HEADS
    outs = []
    for h in range(B_HEADS):
        kvh = h // grp
        qh = q_ref[0, h * HEAD_DIM:(h + 1) * HEAD_DIM, :]
        rhs = jnp.concatenate([qh, zeros] if kvh == 0 else [zeros, qh], axis=0)
        s = jnp.dot(k, rhs, preferred_element_type=F32) + bias_ref[h] + pen
        sink = sink_ref[h]
        m = jnp.maximum(jnp.max(s, axis=0, keepdims=True), sink)
        e = jnp.exp(s - m)
        l = jnp.sum(e, axis=0, keepdims=True) + jnp.exp(sink - m)
        vt = vfull[kvh * HEAD_DIM:(kvh + 1) * HEAD_DIM, :]
        outs.append(jnp.dot(vt, e.astype(BF16), preferred_element_type=F32) / l)
    o_ref[...] = jnp.concatenate(outs, axis=0).T


def _b_call(sink, qbt, kb, vbt, batch, seq):
    t = batch * seq
    n_q = seq // B_TQ
    nkb = seq // B_HALF
    r = B_TQ // B_HALF
    kvw = B_KV_HEADS * HEAD_DIM
    prev_i = lambda b, i: b * nkb + jnp.maximum(i * r - 1, 0)
    next_i = lambda b, i: b * nkb + jnp.minimum((i + 1) * r, nkb - 1)
    bias = jnp.asarray(_b_bias())
    return pl.pallas_call(
        functools.partial(_b_kernel, n_q=n_q),
        grid=(batch, n_q),
        in_specs=[pl.BlockSpec(memory_space=pltpu.SMEM),
                  pl.BlockSpec((1, B_WIDTH, B_TQ), lambda b, i: (b * n_q + i, 0, 0)),
                  pl.BlockSpec((B_HALF, kvw), lambda b, i: (prev_i(b, i), 0)),
                  pl.BlockSpec((B_TQ, kvw), lambda b, i: (b * n_q + i, 0)),
                  pl.BlockSpec((B_HALF, kvw), lambda b, i: (next_i(b, i), 0)),
                  pl.BlockSpec((1, kvw, B_HALF), lambda b, i: (prev_i(b, i), 0, 0)),
                  pl.BlockSpec((r, kvw, B_HALF), lambda b, i: (b * n_q + i, 0, 0)),
                  pl.BlockSpec((1, kvw, B_HALF), lambda b, i: (next_i(b, i), 0, 0)),
                  _const_spec(bias.shape)],
        out_specs=pl.BlockSpec((B_TQ, B_WIDTH), lambda b, i: (b * n_q + i, 0)),
        out_shape=jax.ShapeDtypeStruct((t, B_WIDTH), F32),
        scratch_shapes=[pltpu.VMEM((B_TQ + 2 * B_HALF, kvw), BF16),
                        pltpu.VMEM((kvw, B_TQ + 2 * B_HALF), BF16)],
        compiler_params=pltpu.CompilerParams(dimension_semantics=("parallel", "parallel"),
                                             vmem_limit_bytes=VMEM_LIMIT),
        name="mixer_b",
    )(sink, qbt, kb, kb, kb, vbt, vbt, vbt, bias)


def _c_bias():
    slopes = _alibi_slopes(C_HEADS)
    d = (np.arange(C_TK)[:, None] - np.arange(C_TQ)[None, :]).astype(np.float32)
    tiles = [d, -d]
    for part in range(C_TK // C_TQ):
        tiles.append(-np.abs(d - np.float32(part * C_TQ)))
    return (slopes[:, None, None, None] * np.stack(tiles)[None]).astype(np.float32)


def _c_kernel(scal_ref, lamv_ref, q_ref, k_ref, v_ref, bias_ref, g_ref, o_ref, acc1, acc2,
              *, n_kv):
    h = pl.program_id(1)
    i = pl.program_id(2)
    ratio = C_TK // C_TQ
    jd = i // ratio
    slope = scal_ref[h]
    lam_init = scal_ref[C_HEADS]
    q = q_ref[0]
    zeros = jnp.zeros((HEAD_DIM, C_TQ), BF16)
    rhs1 = jnp.concatenate([q[:HEAD_DIM], zeros], axis=0)
    rhs2 = jnp.concatenate([zeros, q[HEAD_DIM:]], axis=0)
    acc1[...] = jnp.zeros_like(acc1)
    acc2[...] = jnp.zeros_like(acc2)

    def one_map(s, c, m, l, acc, v):
        msh = m - c
        mn = jnp.maximum(msh, jnp.max(s, axis=0, keepdims=True))
        alpha = jnp.exp(msh - mn)
        e = jnp.exp(s - mn)
        l = alpha * l + jnp.sum(e, axis=0, keepdims=True)
        acc[...] = alpha * acc[...] + jnp.dot(v, e.astype(BF16), preferred_element_type=F32)
        return mn + c, l

    def block(j, carry, bias, c):
        m1, l1, m2, l2 = carry
        k = k_ref[pl.ds(pl.multiple_of(j * C_TK, C_TK), C_TK), :]
        v = v_ref[j]
        s1 = jnp.dot(k, rhs1, preferred_element_type=F32) + bias
        m1, l1 = one_map(s1, c, m1, l1, acc1, v)
        s2 = jnp.dot(k, rhs2, preferred_element_type=F32) + bias
        m2, l2 = one_map(s2, c, m2, l2, acc2, v)
        return m1, l1, m2, l2

    def before(j, carry):
        c = -slope * (C_TQ * i - C_TK * j).astype(F32)
        return block(j, carry, bias_ref[0, 0], c)

    def after(j, carry):
        c = -slope * (C_TK * j - C_TQ * i).astype(F32)
        return block(j, carry, bias_ref[0, 1], c)

    neg = jnp.full((1, C_TQ), NEG_INF, F32)
    zero = jnp.zeros((1, C_TQ), F32)
    carry = (neg, zero, neg, zero)
    carry = lax.fori_loop(0, jd, before, carry)
    carry = block(jd, carry, bias_ref[0, 2 + i % ratio], jnp.float32(0.0))
    m1, l1, m2, l2 = lax.fori_loop(jd + 1, n_kv, after, carry)

    lv = lamv_ref[...]
    lam = (jnp.exp(jnp.sum(lv[0:1] * lv[1:2], axis=1, keepdims=True))
           - jnp.exp(jnp.sum(lv[2:3] * lv[3:4], axis=1, keepdims=True)) + lam_init)
    a = acc1[...] / l1 - lam * (acc2[...] / l2)
    ms = jnp.mean(a * a, axis=0, keepdims=True)
    y = a * lax.rsqrt(ms + RMS_EPS) * g_ref[...] * (1.0 - lam_init)
    o_ref[...] = y.T


def _c_call(scal, lamv, subln_g, qct, kc, vct, batch, seq):
    t = batch * seq
    n_q = seq // C_TQ
    n_kv = seq // C_TK
    kw = 2 * HEAD_DIM
    bias = jnp.asarray(_c_bias())
    g = jnp.broadcast_to(subln_g.astype(F32)[:, None], (C_VDIM, C_TQ))
    return pl.pallas_call(
        functools.partial(_c_kernel, n_kv=n_kv),
        grid=(batch, C_HEADS, n_q),
        in_specs=[pl.BlockSpec(memory_space=pltpu.SMEM),
                  _const_spec((4, HEAD_DIM)),
                  pl.BlockSpec((1, kw, C_TQ), lambda b, h, i: (b * n_q + i, h, 0)),
                  pl.BlockSpec((seq, kw), lambda b, h, i: (b, h)),
                  pl.BlockSpec((n_kv, C_VDIM, C_TK), lambda b, h, i: (b, h, 0)),
                  pl.BlockSpec((1,) + bias.shape[1:], lambda b, h, i: (h, 0, 0, 0)),
                  _const_spec((C_VDIM, C_TQ))],
        out_specs=pl.BlockSpec((C_TQ, C_VDIM), lambda b, h, i: (b * n_q + i, h)),
        out_shape=jax.ShapeDtypeStruct((t, C_WIDTH), F32),
        scratch_shapes=[pltpu.VMEM((C_VDIM, C_TQ), F32), pltpu.VMEM((C_VDIM, C_TQ), F32)],
        compiler_params=pltpu.CompilerParams(
            dimension_semantics=("parallel", "parallel", "parallel"), vmem_limit_bytes=VMEM_LIMIT),
        name="mixer_c",
    )(scal, lamv, qct, kc, vct, bias, g)


def _sigmoid(x):
    return 1.0 / (1.0 + jnp.exp(-x))


def _post_kernel(x_ref, g_ref, oa0_ref, oa1_ref, oa2_ref, la0_ref, la1_ref, la2_ref, ob_ref, oc_ref,
                 wg_ref, woa_ref, wob_ref, woc_ref, wout_ref, fg_ref, y_ref, *, final):
    x = x_ref[...]
    h = _rms(x, g_ref[...]).astype(BF16)

    def gate(c0, n):
        return jnp.dot(h, wg_ref[:, c0:c0 + n], preferred_element_type=F32)

    def silu_gated(o, c0):
        ga = gate(c0, o.shape[1])
        return (o * (ga * _sigmoid(ga))).astype(BF16)

    l0, l1, l2 = la0_ref[...], la1_ref[...], la2_ref[...]
    m = jnp.maximum(jnp.maximum(l0, l1), l2)
    e0, e1, e2 = jnp.exp(l0 - m), jnp.exp(l1 - m), jnp.exp(l2 - m)
    oa = (e0 * oa0_ref[...] + e1 * oa1_ref[...] + e2 * oa2_ref[...]) / (e0 + e1 + e2)

    ya = jnp.dot(silu_gated(oa, 0), woa_ref[...], preferred_element_type=F32)
    yb = jnp.dot(silu_gated(ob_ref[...], A_WIDTH), wob_ref[...], preferred_element_type=F32)
    yc = jnp.dot(silu_gated(oc_ref[...], A_WIDTH + B_WIDTH), woc_ref[...], preferred_element_type=F32)
    g0 = A_WIDTH + B_WIDTH + C_WIDTH
    mixed = (_sigmoid(gate(g0, D_MODEL)) * ya + _sigmoid(gate(g0 + D_MODEL, D_MODEL)) * yb
             + _sigmoid(gate(g0 + 2 * D_MODEL, D_MODEL)) * yc)
    y = x + jnp.dot(mixed.astype(BF16), wout_ref[...], preferred_element_type=F32)
    if final:
        y = _rms(y, fg_ref[...])
    y_ref[...] = y


def _post_call(x, g, oa, la, ob, oc, w, final_g, final):
    t = x.shape[0]
    tm = POST_TM
    tok = lambda n: pl.BlockSpec((tm, n), lambda i: (i, 0))
    return pl.pallas_call(
        functools.partial(_post_kernel, final=final),
        grid=(t // tm,),
        in_specs=[tok(D_MODEL), _const_spec((1, D_MODEL))] + [tok(A_WIDTH)] * 8
                 + [_const_spec(w["g"].shape), _const_spec(w["oa"].shape), _const_spec(w["ob"].shape),
                    _const_spec(w["oc"].shape), _const_spec(w["out"].shape), _const_spec((1, D_MODEL))],
        out_specs=tok(D_MODEL),
        out_shape=jax.ShapeDtypeStruct((t, D_MODEL), F32),
        compiler_params=pltpu.CompilerParams(dimension_semantics=("parallel",),
                                             vmem_limit_bytes=VMEM_LIMIT),
        name="post_final" if final else "post",
    )(x, g, oa[0], oa[1], oa[2], la[0], la[1], la[2], ob, oc,
      w["g"], w["oa"], w["ob"], w["oc"], w["out"], final_g)


def _layer_weights(w_in, w_oa, w_ob, w_oc, w_out):
    col = lambda idx: w_in[:, IN_OFFSETS[idx]:IN_OFFSETS[idx + 1]]
    scale = HEAD_DIM ** -0.5
    ft = jnp.concatenate([col(4) * scale, col(6), col(8) * scale, col(10)], axis=1).T
    return {
        "qa": (col(0) * scale).astype(BF16), "ka": col(1).astype(BF16), "va": col(2).astype(BF16),
        "kb": col(5).astype(BF16), "kc": col(9).astype(BF16), "ft": ft.astype(BF16),
        "g": jnp.concatenate([col(3), col(7), col(11), col(12)], axis=1).astype(BF16),
        "oa": w_oa.astype(BF16), "ob": w_ob.astype(BF16), "oc": w_oc.astype(BF16),
        "out": w_out.astype(BF16),
    }


def _trunk(x3, layers, final_g):
    batch, seq, _ = x3.shape
    x = x3.reshape(batch * seq, D_MODEL)
    fg = final_g.astype(F32).reshape(1, D_MODEL)
    for li, lw in enumerate(layers):
        w = lw["w"]
        qa, ka, va, kb, kc, qbt, vbt, qct, vct = _proj_call(x, lw["norm_g"], w)
        oa, la = zip(*[_a_call(qa, ka, va, gi, batch, seq) for gi in range(A_GROUPS)])
        ob = _b_call(lw["sink"], qbt, kb, vbt, batch, seq)
        oc = _c_call(lw["scal"], lw["lamv"], lw["subln_g"], qct, kc, vct, batch, seq)
        x = _post_call(x, lw["norm_g"], oa, la, ob, oc, w, fg, final=(li == len(layers) - 1))
    return x.reshape(batch, seq, D_MODEL)


def kernel(x_prompt, x_sample, norm_g, w_in, w_oa, w_ob, w_oc, w_out, b_sink, lam_q1, lam_k1, lam_q2, lam_k2, c_subln_g, final_norm_g):
    c_slopes = jnp.asarray(_alibi_slopes(C_HEADS))
    layers = []
    for l in range(DEPTH):
        lam_init = 0.8 - 0.6 * math.exp(-0.3 * l)
        layers.append({
            "w": _layer_weights(w_in[l], w_oa[l], w_ob[l], w_oc[l], w_out[l]),
            "norm_g": norm_g[l].astype(F32).reshape(1, D_MODEL),
            "sink": b_sink[l].astype(F32),
            "scal": jnp.concatenate([c_slopes, jnp.full((1,), lam_init, F32)]),
            "lamv": jnp.stack([lam_q1[l], lam_k1[l], lam_q2[l], lam_k2[l]]).astype(F32),
            "subln_g": c_subln_g[l],
        })
    return (_trunk(x_prompt, layers, final_norm_g), _trunk(x_sample, layers, final_norm_g))
```

```python
import functools
import math

import numpy as np
import jax
import jax.numpy as jnp
from jax import lax
from jax.experimental import pallas as pl
from jax.experimental.pallas import tpu as pltpu

F32 = jnp.float32
BF16 = jnp.bfloat16

D_MODEL = 1024
DEPTH = 4
HEAD_DIM = 64
A_PATTERNS = ((128, 1), (512, 4), (2048, 16))
A_GROUPS = 3
A_HEADS = 8
A_WIDTH = A_HEADS * HEAD_DIM
A_HALF = 64
B_HEADS = 8
B_KV_HEADS = 2
B_HALF = 128
B_WIDTH = B_HEADS * HEAD_DIM
C_HEADS = 4
C_VDIM = 2 * HEAD_DIM
C_WIDTH = C_HEADS * C_VDIM
RMS_EPS = 1e-6
NEG_INF = -1e30
IN_SIZES = (
    A_GROUPS * A_WIDTH, A_GROUPS * A_WIDTH, A_GROUPS * A_WIDTH, A_WIDTH,
    B_WIDTH, B_KV_HEADS * HEAD_DIM, B_KV_HEADS * HEAD_DIM, B_WIDTH,
    2 * C_HEADS * HEAD_DIM, 2 * C_HEADS * HEAD_DIM, C_WIDTH, C_WIDTH,
    3 * D_MODEL,
)
IN_OFFSETS = tuple(int(c) for c in np.cumsum((0,) + IN_SIZES))

PROJ_TM = 512
POST_TM = 256
A_TQ = 128
A_STEP = 512
B_TQ = 256
C_TQ = 256
C_TK = 512
C_UNIT = 256
C_ONES_ROWS = 16
LOG2E = 1.4426950408889634
LANES = 128
VMEM_LIMIT = 56 * 2**20

N_FEAT = B_WIDTH + B_KV_HEADS * HEAD_DIM + 2 * C_HEADS * HEAD_DIM + C_WIDTH


def _alibi_slopes(n):
    return np.asarray([2.0 ** (-8.0 * (i + 1) / n) for i in range(n)], dtype=np.float32)


def _rms(x, g):
    ms = jnp.mean(x * x, axis=-1, keepdims=True)
    return x * lax.rsqrt(ms + RMS_EPS) * g


def _const_spec(shape):
    nd = len(shape)
    return pl.BlockSpec(shape, lambda *_: (0,) * nd, pipeline_mode=pl.Buffered(1))


def _proj_kernel(x_ref, g_ref, wa0_ref, wa1_ref, wa2_ref, wkb_ref, wkc_ref, wft_ref,
                 a0_ref, a1_ref, a2_ref, kb_ref, kc_ref, qbt_ref, vbt_ref, qct_ref, vct_ref, h_scr):
    tm = x_ref.shape[0]
    h32 = _rms(x_ref[...], g_ref[...])
    h = h32.astype(BF16)
    a0_ref[0, 0] = jnp.dot(h, wa0_ref[...], preferred_element_type=F32).astype(BF16)
    n_chunks = h_scr.shape[0]
    for c in range(n_chunks):
        h_scr[c] = h32[:, c * LANES:(c + 1) * LANES]
    for w_ref, o_ref, (_, dil) in ((wa1_ref, a1_ref, A_PATTERNS[1]), (wa2_ref, a2_ref, A_PATTERNS[2])):
        n = tm // dil
        hp = jnp.concatenate(
            [jnp.concatenate([h_scr[c, pl.ds(r, n, stride=dil), :] for c in range(n_chunks)], axis=1)
             for r in range(dil)], axis=0).astype(BF16)
        res = jnp.dot(hp, w_ref[...], preferred_element_type=F32).astype(BF16)
        for r in range(dil):
            o_ref[0, r] = res[r * n:(r + 1) * n]
    for w_ref, o_ref in ((wkb_ref, kb_ref), (wkc_ref, kc_ref)):
        o_ref[...] = jnp.dot(h, w_ref[...], preferred_element_type=F32).astype(BF16)
    ft = lax.dot_general(wft_ref[...], h, (((1,), (1,)), ((), ())),
                         preferred_element_type=F32).astype(BF16)
    r0 = 0
    for o_ref, rows in ((qbt_ref, B_WIDTH), (vbt_ref, B_KV_HEADS * HEAD_DIM),
                        (qct_ref, 2 * C_HEADS * HEAD_DIM), (vct_ref, C_WIDTH)):
        nblk, _, width = o_ref.shape
        for c in range(nblk):
            o_ref[c] = ft[r0:r0 + rows, c * width:(c + 1) * width]
        r0 += rows


def _proj_call(x, g, w, batch, seq):
    t = x.shape[0]
    tm = PROJ_TM
    nt = seq // tm
    tok = lambda n: pl.BlockSpec((tm, n), lambda i: (i, 0))
    feat = lambda rows, width: pl.BlockSpec((tm // width, rows, width), lambda i: (i, 0, 0))
    a_cols = 3 * A_WIDTH
    res = lambda dil: pl.BlockSpec((1, dil, tm // dil, a_cols), lambda i: (i // nt, 0, i % nt, 0))
    a_shape = lambda dil: jax.ShapeDtypeStruct((batch, dil, seq // dil, a_cols), BF16)
    dils = [d for _, d in A_PATTERNS]
    out_shape = (
        a_shape(dils[0]), a_shape(dils[1]), a_shape(dils[2]),
        jax.ShapeDtypeStruct((t, B_KV_HEADS * HEAD_DIM), BF16),
        jax.ShapeDtypeStruct((t, 2 * C_HEADS * HEAD_DIM), BF16),
        jax.ShapeDtypeStruct((t // B_TQ, B_WIDTH, B_TQ), BF16),
        jax.ShapeDtypeStruct((t // B_HALF, B_KV_HEADS * HEAD_DIM, B_HALF), BF16),
        jax.ShapeDtypeStruct((t // C_TQ, 2 * C_HEADS * HEAD_DIM, C_TQ), BF16),
        jax.ShapeDtypeStruct((t // C_TK, C_WIDTH, C_TK), BF16),
    )
    return pl.pallas_call(
        _proj_kernel,
        grid=(t // tm,),
        in_specs=[tok(D_MODEL), _const_spec((1, D_MODEL)),
                  _const_spec(w["a0"].shape), _const_spec(w["a1"].shape), _const_spec(w["a2"].shape),
                  _const_spec(w["kb"].shape), _const_spec(w["kc"].shape), _const_spec(w["ft"].shape)],
        out_specs=(res(dils[0]), res(dils[1]), res(dils[2]), tok(B_KV_HEADS * HEAD_DIM),
                   tok(2 * C_HEADS * HEAD_DIM),
                   feat(B_WIDTH, B_TQ), feat(B_KV_HEADS * HEAD_DIM, B_HALF),
                   feat(2 * C_HEADS * HEAD_DIM, C_TQ), feat(C_WIDTH, C_TK)),
        out_shape=out_shape,
        scratch_shapes=[pltpu.VMEM((D_MODEL // LANES, tm, LANES), F32)],
        compiler_params=pltpu.CompilerParams(dimension_semantics=("parallel",),
                                             vmem_limit_bytes=VMEM_LIMIT),
        name="proj",
    )(x, g, w["a0"], w["a1"], w["a2"], w["kb"], w["kc"], w["ft"])


def _a_bias(gi):
    dil = A_PATTERNS[gi][1]
    slopes = _alibi_slopes(A_GROUPS * A_HEADS).reshape(A_GROUPS, A_HEADS)[gi]
    rel = (np.arange(A_TQ + 2 * A_HALF)[None, :] - A_HALF) - np.arange(A_TQ)[:, None]
    dist = np.abs(rel).astype(np.float32) * np.float32(dil)
    bias = -slopes[:, None, None] * dist[None]
    return np.where(np.abs(rel)[None] <= A_HALF, bias, np.float32(NEG_INF)).astype(np.float32)


def _a_kernel(q_ref, kp_ref, kc_ref, kn_ref, vp_ref, vc_ref, vn_ref, bias_ref,
              o_ref, lse_ref, kfull, vfull, *, step, n_steps):
    ui = pl.program_id(2)
    tkw = A_TQ + 2 * A_HALF
    kfull[0:A_HALF] = kp_ref[0, 0]
    kfull[A_HALF:A_HALF + step] = kc_ref[0, 0]
    kfull[A_HALF + step:] = kn_ref[0, 0]
    vfull[0:A_HALF] = vp_ref[0, 0]
    vfull[A_HALF:A_HALF + step] = vc_ref[0, 0]
    vfull[A_HALF + step:] = vn_ref[0, 0]
    n_sb = step // A_TQ
    col = lax.broadcasted_iota(jnp.int32, (A_TQ, tkw), 1)
    lane = lax.broadcasted_iota(jnp.int32, (tkw, 2 * HEAD_DIM), 1)
    lane_o = lax.broadcasted_iota(jnp.int32, (A_TQ, 2 * HEAD_DIM), 1)
    for sb in range(n_sb):
        pen = None
        if sb == 0:
            pen = jnp.where((col < A_HALF) & (ui == 0), NEG_INF, 0.0)
        if sb == n_sb - 1:
            pen_hi = jnp.where((col >= A_TQ + A_HALF) & (ui == n_steps - 1), NEG_INF, 0.0)
            pen = pen_hi if pen is None else pen + pen_hi
        r0 = sb * A_TQ
        for p in range(A_HEADS // 2):
            c0 = p * 2 * HEAD_DIM
            qp = q_ref[0, 0, r0:r0 + A_TQ, c0:c0 + 2 * HEAD_DIM]
            kw = kfull[r0:r0 + tkw, c0:c0 + 2 * HEAD_DIM]
            vw = vfull[r0:r0 + tkw, c0:c0 + 2 * HEAD_DIM]
            o_pair = None
            lse_pair = None
            for hh in range(2):
                in_half = (lane >= hh * HEAD_DIM) & (lane < (hh + 1) * HEAD_DIM)
                kz = jnp.where(in_half, kw, jnp.zeros_like(kw))
                vz = jnp.where(in_half, vw, jnp.zeros_like(vw))
                s = lax.dot_general(qp, kz, (((1,), (1,)), ((), ())), preferred_element_type=F32)
                s = s + bias_ref[2 * p + hh]
                if pen is not None:
                    s = s + pen
                m = jnp.max(s, axis=1, keepdims=True)
                e = jnp.exp(s - m)
                l = jnp.sum(e, axis=1, keepdims=True)
                o = jnp.dot(e.astype(BF16), vz, preferred_element_type=F32) / l
                lse = jnp.broadcast_to(m + jnp.log(l), (A_TQ, 2 * HEAD_DIM))
                if hh == 0:
                    o_pair, lse_pair = o, lse
                else:
                    o_pair = o_pair + o
                    lse_pair = jnp.where(lane_o < HEAD_DIM, lse_pair, lse)
            o_ref[0, 0, r0:r0 + A_TQ, c0:c0 + 2 * HEAD_DIM] = o_pair
            lse_ref[0, 0, r0:r0 + A_TQ, c0:c0 + 2 * HEAD_DIM] = lse_pair


def _a_call(qkv, gi):
    batch, dil, sub, _ = qkv.shape
    step = min(A_STEP, sub)
    n_steps = sub // step
    hb = step // A_HALF
    nhb = sub // A_HALF
    cur = lambda c: pl.BlockSpec((1, 1, step, A_WIDTH), lambda b, r, u: (b, r, u, c))
    prev = lambda c: pl.BlockSpec((1, 1, A_HALF, A_WIDTH),
                                  lambda b, r, u: (b, r, jnp.maximum(u * hb - 1, 0), c))
    nxt = lambda c: pl.BlockSpec((1, 1, A_HALF, A_WIDTH),
                                 lambda b, r, u: (b, r, jnp.minimum((u + 1) * hb, nhb - 1), c))
    bias = jnp.asarray(_a_bias(gi))
    return pl.pallas_call(
        functools.partial(_a_kernel, step=step, n_steps=n_steps),
        grid=(batch, dil, n_steps),
        in_specs=[cur(0), prev(1), cur(1), nxt(1), prev(2), cur(2), nxt(2), _const_spec(bias.shape)],
        out_specs=(cur(0), cur(0)),
        out_shape=(jax.ShapeDtypeStruct((batch, dil, sub, A_WIDTH), F32),
                   jax.ShapeDtypeStruct((batch, dil, sub, A_WIDTH), F32)),
        scratch_shapes=[pltpu.VMEM((step + 2 * A_HALF, A_WIDTH), BF16),
                        pltpu.VMEM((step + 2 * A_HALF, A_WIDTH), BF16)],
        compiler_params=pltpu.CompilerParams(
            dimension_semantics=("parallel", "parallel", "parallel"), vmem_limit_bytes=VMEM_LIMIT),
        name=f"mixer_a{gi}",
    )(qkv, qkv, qkv, qkv, qkv, qkv, qkv, bias)


def _b_bias():
    slopes = _alibi_slopes(B_HEADS)
    rel = (np.arange(B_TQ + 2 * B_HALF)[:, None] - B_HALF) - np.arange(B_TQ)[None, :]
    bias = -slopes[:, None, None] * np.abs(rel).astype(np.float32)[None]
    return np.where(np.abs(rel)[None] <= B_HALF, bias, np.float32(NEG_INF)).astype(np.float32)


def _b_kernel(sink_ref, q_ref, kp_ref, kc_ref, kn_ref, vp_ref, vc_ref, vn_ref, bias_ref,
              o_ref, kfull, vfull, *, n_q):
    i = pl.program_id(1)
    tk = B_TQ + 2 * B_HALF
    kfull[0:B_HALF] = kp_ref[...]
    kfull[B_HALF:B_HALF + B_TQ] = kc_ref[...]
    kfull[B_HALF + B_TQ:] = kn_ref[...]
    vfull[:, 0:B_HALF] = vp_ref[0]
    for c in range(B_TQ // B_HALF):
        vfull[:, (c + 1) * B_HALF:(c + 2) * B_HALF] = vc_ref[c]
    vfull[:, B_HALF + B_TQ:] = vn_ref[0]
    row = lax.broadcasted_iota(jnp.int32, (tk, B_TQ), 0)
    pen = (jnp.where((row < B_HALF) & (i == 0), NEG_INF, 0.0)
           + jnp.where((row >= B_HALF + B_TQ) & (i == n_q - 1), NEG_INF, 0.0))
    k = kfull[...]
    zeros = jnp.zeros((HEAD_DIM, B_TQ), BF16)
    grp = B_HEADS // B_KV_HEADS
    outs = []
    for h in range(B_HEADS):
        kvh = h // grp
        qh = q_ref[0, h * HEAD_DIM:(h + 1) * HEAD_DIM, :]
        rhs = jnp.concatenate([qh, zeros] if kvh == 0 else [zeros, qh], axis=0)
        s = jnp.dot(k, rhs, preferred_element_type=F32) + bias_ref[h] + pen
        sink = sink_ref[h]
        m = jnp.maximum(jnp.max(s, axis=0, keepdims=True), sink)
        e = jnp.exp(s - m)
        l = jnp.sum(e, axis=0, keepdims=True) + jnp.exp(sink - m)
        vt = vfull[kvh * HEAD_DIM:(kvh + 1) * HEAD_DIM, :]
        outs.append(jnp.dot(vt, e.astype(BF16), preferred_element_type=F32) / l)
    o_ref[...] = jnp.concatenate(outs, axis=0).T


def _b_call(sink, qbt, kb, vbt, batch, seq):
    t = batch * seq
    n_q = seq // B_TQ
    nkb = seq // B_HALF
    r = B_TQ // B_HALF
    kvw = B_KV_HEADS * HEAD_DIM
    prev_i = lambda b, i: b * nkb + jnp.maximum(i * r - 1, 0)
    next_i = lambda b, i: b * nkb + jnp.minimum((i + 1) * r, nkb - 1)
    bias = jnp.asarray(_b_bias())
    return pl.pallas_call(
        functools.partial(_b_kernel, n_q=n_q),
        grid=(batch, n_q),
        in_specs=[pl.BlockSpec(memory_space=pltpu.SMEM),
                  pl.BlockSpec((1, B_WIDTH, B_TQ), lambda b, i: (b * n_q + i, 0, 0)),
                  pl.BlockSpec((B_HALF, kvw), lambda b, i: (prev_i(b, i), 0)),
                  pl.BlockSpec((B_TQ, kvw), lambda b, i: (b * n_q + i, 0)),
                  pl.BlockSpec((B_HALF, kvw), lambda b, i: (next_i(b, i), 0)),
                  pl.BlockSpec((1, kvw, B_HALF), lambda b, i: (prev_i(b, i), 0, 0)),
                  pl.BlockSpec((r, kvw, B_HALF), lambda b, i: (b * n_q + i, 0, 0)),
                  pl.BlockSpec((1, kvw, B_HALF), lambda b, i: (next_i(b, i), 0, 0)),
                  _const_spec(bias.shape)],
        out_specs=pl.BlockSpec((B_TQ, B_WIDTH), lambda b, i: (b * n_q + i, 0)),
        out_shape=jax.ShapeDtypeStruct((t, B_WIDTH), F32),
        scratch_shapes=[pltpu.VMEM((B_TQ + 2 * B_HALF, kvw), BF16),
                        pltpu.VMEM((kvw, B_TQ + 2 * B_HALF), BF16)],
        compiler_params=pltpu.CompilerParams(dimension_semantics=("parallel", "parallel"),
                                             vmem_limit_bytes=VMEM_LIMIT),
        name="mixer_b",
    )(sink, qbt, kb, kb, kb, vbt, vbt, vbt, bias)


def _c_bias():
    slopes = _alibi_slopes(C_HEADS).astype(np.float64) * LOG2E
    d = (np.arange(C_TK)[:, None] - np.arange(C_TQ)[None, :]).astype(np.float64)
    tiles = [d, -d]
    for part in range(C_TK // C_TQ):
        tiles.append(-np.abs(d - part * C_TQ))
    return (slopes[:, None, None, None] * np.stack(tiles)[None]).astype(np.float32)


def _c_kernel(scal_ref, lamv_ref, q_ref, k_ref, v_ref, bias_ref, g_ref, o_ref,
              s_even, s_odd, acc1, acc2, *, n_kv):
    h = pl.program_id(1)
    i = pl.program_id(2)
    ratio = C_TK // C_TQ
    n_units = C_TK // C_UNIT
    jd = i // ratio
    slope2 = scal_ref[h]
    lam_init = scal_ref[C_HEADS]
    q = q_ref[0]
    zeros = jnp.zeros((HEAD_DIM, C_TQ), BF16)
    rhs = (jnp.concatenate([q[:HEAD_DIM], zeros], axis=0),
           jnp.concatenate([zeros, q[HEAD_DIM:]], axis=0))
    accs = (acc1, acc2)
    ones_rows = (lax.broadcasted_iota(jnp.int32, (C_ONES_ROWS, C_UNIT), 0) == 0).astype(BF16)
    acc1[...] = jnp.zeros_like(acc1)
    acc2[...] = jnp.zeros_like(acc2)
    neg = jnp.full((1, C_TQ), NEG_INF, F32)

    def bias_tile(j):
        return jnp.where(j < jd, 0, jnp.where(j > jd, 1, 2 + i % ratio))

    def block_offset(j):
        return jnp.where(j == jd, 0.0, -slope2 * jnp.abs(C_TQ * i - C_TK * j).astype(F32))

    def score_unit(j, tile, u, s_scr, bm):
        r0 = u * C_UNIT
        k = k_ref[pl.ds(pl.multiple_of(j * C_TK + r0, C_UNIT), C_UNIT), :]
        bias = bias_ref[0, tile, r0:r0 + C_UNIT, :]
        out = []
        for mp in range(2):
            s = jnp.dot(k, rhs[mp], preferred_element_type=F32) + bias
            s_scr[mp, r0:r0 + C_UNIT, :] = s
            out.append(jnp.maximum(bm[mp], jnp.max(s, axis=0, keepdims=True)))
        return tuple(out)

    def value_unit(j, u, s_scr, mn, pvs):
        r0 = u * C_UNIT
        v = jnp.concatenate([v_ref[j, :, r0:r0 + C_UNIT], ones_rows], axis=0)
        out = []
        for mp in range(2):
            e = jnp.exp2(s_scr[mp, r0:r0 + C_UNIT, :] - mn[mp])
            pv = jnp.dot(v, e.astype(BF16), preferred_element_type=F32)
            out.append(pv if pvs[mp] is None else pvs[mp] + pv)
        return tuple(out)

    def stage(j_cur, s_cur, bm_cur, m, j_next, s_next):
        c = block_offset(j_cur)
        mn, alpha, m_new = [], [], []
        for mp in range(2):
            msh = m[mp] - c
            x = jnp.maximum(msh, bm_cur[mp])
            mn.append(x)
            alpha.append(jnp.exp2(msh - x))
            m_new.append(x + c)
        bm_next = (neg, neg)
        pvs = (None, None)
        for u in range(n_units):
            if j_next is not None:
                bm_next = score_unit(j_next, bias_tile(j_next), u, s_next, bm_next)
            pvs = value_unit(j_cur, u, s_cur, mn, pvs)
        for mp in range(2):
            accs[mp][...] = alpha[mp] * accs[mp][...] + pvs[mp]
        return bm_next, tuple(m_new)

    bm = (neg, neg)
    for u in range(n_units):
        bm = score_unit(0, bias_tile(0), u, s_even, bm)

    def pair(jj, carry):
        bm, m = carry
        j = 2 * jj
        bm, m = stage(j, s_even, bm, m, j + 1, s_odd)
        return stage(j + 1, s_odd, bm, m, j + 2, s_even)

    bm, m = lax.fori_loop(0, n_kv // 2 - 1, pair, (bm, (neg, neg)))
    bm, m = stage(n_kv - 2, s_even, bm, m, n_kv - 1, s_odd)
    stage(n_kv - 1, s_odd, bm, m, None, None)

    lv = lamv_ref[...]
    lam = (jnp.exp(jnp.sum(lv[0:1] * lv[1:2], axis=1, keepdims=True))
           - jnp.exp(jnp.sum(lv[2:3] * lv[3:4], axis=1, keepdims=True)) + lam_init)
    a1 = acc1[...]
    a2 = acc2[...]
    a = a1[:C_VDIM] / a1[C_VDIM:C_VDIM + 1] - lam * (a2[:C_VDIM] / a2[C_VDIM:C_VDIM + 1])
    ms = jnp.mean(a * a, axis=0, keepdims=True)
    y = a * lax.rsqrt(ms + RMS_EPS) * g_ref[...] * (1.0 - lam_init)
    o_ref[...] = y.T


def _c_call(scal, lamv, subln_g, qct, kc, vct, batch, seq):
    t = batch * seq
    n_q = seq // C_TQ
    n_kv = seq // C_TK
    kw = 2 * HEAD_DIM
    bias = jnp.asarray(_c_bias())
    g = jnp.broadcast_to(subln_g.astype(F32)[:, None], (C_VDIM, C_TQ))
    return pl.pallas_call(
        functools.partial(_c_kernel, n_kv=n_kv),
        grid=(batch, C_HEADS, n_q),
        in_specs=[pl.BlockSpec(memory_space=pltpu.SMEM),
                  _const_spec((4, HEAD_DIM)),
                  pl.BlockSpec((1, kw, C_TQ), lambda b, h, i: (b * n_q + i, h, 0)),
                  pl.BlockSpec((seq, kw), lambda b, h, i: (b, h)),
                  pl.BlockSpec((n_kv, C_VDIM, C_TK), lambda b, h, i: (b, h, 0)),
                  pl.BlockSpec((1,) + bias.shape[1:], lambda b, h, i: (h, 0, 0, 0)),
                  _const_spec((C_VDIM, C_TQ))],
        out_specs=pl.BlockSpec((C_TQ, C_VDIM), lambda b, h, i: (b * n_q + i, h)),
        out_shape=jax.ShapeDtypeStruct((t, C_WIDTH), F32),
        scratch_shapes=[pltpu.VMEM((2, C_TK, C_TQ), F32), pltpu.VMEM((2, C_TK, C_TQ), F32),
                        pltpu.VMEM((C_VDIM + C_ONES_ROWS, C_TQ), F32),
                        pltpu.VMEM((C_VDIM + C_ONES_ROWS, C_TQ), F32)],
        compiler_params=pltpu.CompilerParams(
            dimension_semantics=("parallel", "parallel", "parallel"), vmem_limit_bytes=VMEM_LIMIT),
        name="mixer_c",
    )(scal, lamv, qct, kc, vct, bias, g)


def _sigmoid(x):
    return 1.0 / (1.0 + jnp.exp(-x))


def _post_kernel(x_ref, g_ref, oa0_ref, oa1_ref, oa2_ref, la0_ref, la1_ref, la2_ref, ob_ref, oc_ref,
                 wg_ref, woa_ref, wob_ref, woc_ref, wout_ref, fg_ref, y_ref, tok_scr, *, final):
    x = x_ref[...]
    h = _rms(x, g_ref[...]).astype(BF16)

    def token_order(ref, slot):
        _, dil, n, width = ref.shape
        if dil == 1:
            return ref[0, 0]
        n_chunks = width // LANES
        for r in range(dil):
            rows = ref[0, r]
            for c in range(n_chunks):
                tok_scr[slot, c, pl.ds(r, n, stride=dil), :] = rows[:, c * LANES:(c + 1) * LANES]
        return jnp.concatenate([tok_scr[slot, c] for c in range(n_chunks)], axis=1)

    def gate(c0, n):
        return jnp.dot(h, wg_ref[:, c0:c0 + n], preferred_element_type=F32)

    def silu_gated(o, c0):
        ga = gate(c0, o.shape[1])
        return (o * (ga * _sigmoid(ga))).astype(BF16)

    l0, l1, l2 = token_order(la0_ref, 0), token_order(la1_ref, 0), token_order(la2_ref, 1)
    o0, o1, o2 = token_order(oa0_ref, 0), token_order(oa1_ref, 2), token_order(oa2_ref, 3)
    m = jnp.maximum(jnp.maximum(l0, l1), l2)
    e0, e1, e2 = jnp.exp(l0 - m), jnp.exp(l1 - m), jnp.exp(l2 - m)
    oa = (e0 * o0 + e1 * o1 + e2 * o2) / (e0 + e1 + e2)

    ya = jnp.dot(silu_gated(oa, 0), woa_ref[...], preferred_element_type=F32)
    yb = jnp.dot(silu_gated(ob_ref[...], A_WIDTH), wob_ref[...], preferred_element_type=F32)
    yc = jnp.dot(silu_gated(oc_ref[...], A_WIDTH + B_WIDTH), woc_ref[...], preferred_element_type=F32)
    g0 = A_WIDTH + B_WIDTH + C_WIDTH
    mixed = (_sigmoid(gate(g0, D_MODEL)) * ya + _sigmoid(gate(g0 + D_MODEL, D_MODEL)) * yb
             + _sigmoid(gate(g0 + 2 * D_MODEL, D_MODEL)) * yc)
    y = x + jnp.dot(mixed.astype(BF16), wout_ref[...], preferred_element_type=F32)
    if final:
        y = _rms(y, fg_ref[...])
    y_ref[...] = y


def _post_call(x, g, oa, la, ob, oc, w, final_g, final, seq):
    t = x.shape[0]
    tm = POST_TM
    nt = seq // tm
    tok = lambda n: pl.BlockSpec((tm, n), lambda i: (i, 0))
    res = [pl.BlockSpec((1, dil, tm // dil, A_WIDTH), lambda i: (i // nt, 0, i % nt, 0))
           for _, dil in A_PATTERNS]
    return pl.pallas_call(
        functools.partial(_post_kernel, final=final),
        grid=(t // tm,),
        in_specs=[tok(D_MODEL), _const_spec((1, D_MODEL))] + res + res + [tok(B_WIDTH), tok(C_WIDTH)]
                 + [_const_spec(w["g"].shape), _const_spec(w["oa"].shape), _const_spec(w["ob"].shape),
                    _const_spec(w["oc"].shape), _const_spec(w["out"].shape), _const_spec((1, D_MODEL))],
        out_specs=tok(D_MODEL),
        out_shape=jax.ShapeDtypeStruct((t, D_MODEL), F32),
        scratch_shapes=[pltpu.VMEM((4, A_WIDTH // LANES, tm, LANES), F32)],
        compiler_params=pltpu.CompilerParams(dimension_semantics=("parallel",),
                                             vmem_limit_bytes=VMEM_LIMIT),
        name="post_final" if final else "post",
    )(x, g, oa[0], oa[1], oa[2], la[0], la[1], la[2], ob, oc,
      w["g"], w["oa"], w["ob"], w["oc"], w["out"], final_g)


def _layer_weights(w_in, w_oa, w_ob, w_oc, w_out):
    col = lambda idx: w_in[:, IN_OFFSETS[idx]:IN_OFFSETS[idx + 1]]
    scale = HEAD_DIM ** -0.5
    ft = jnp.concatenate([col(4) * scale, col(6), col(8) * (scale * LOG2E), col(10)], axis=1).T
    grp = lambda idx, gi: col(idx)[:, gi * A_WIDTH:(gi + 1) * A_WIDTH]
    a_qkv = lambda gi: jnp.concatenate([grp(0, gi) * scale, grp(1, gi), grp(2, gi)], axis=1).astype(BF16)
    return {
        "a0": a_qkv(0), "a1": a_qkv(1), "a2": a_qkv(2),
        "kb": col(5).astype(BF16), "kc": col(9).astype(BF16), "ft": ft.astype(BF16),
        "g": jnp.concatenate([col(3), col(7), col(11), col(12)], axis=1).astype(BF16),
        "oa": w_oa.astype(BF16), "ob": w_ob.astype(BF16), "oc": w_oc.astype(BF16),
        "out": w_out.astype(BF16),
    }


def _trunk(x3, layers, final_g):
    batch, seq, _ = x3.shape
    x = x3.reshape(batch * seq, D_MODEL)
    fg = final_g.astype(F32).reshape(1, D_MODEL)
    for li, lw in enumerate(layers):
        w = lw["w"]
        a0, a1, a2, kb, kc, qbt, vbt, qct, vct = _proj_call(x, lw["norm_g"], w, batch, seq)
        oa, la = zip(*[_a_call(qkv, gi) for gi, qkv in enumerate((a0, a1, a2))])
        ob = _b_call(lw["sink"], qbt, kb, vbt, batch, seq)
        oc = _c_call(lw["scal"], lw["lamv"], lw["subln_g"], qct, kc, vct, batch, seq)
        x = _post_call(x, lw["norm_g"], oa, la, ob, oc, w, fg, final=(li == len(layers) - 1), seq=seq)
    return x.reshape(batch, seq, D_MODEL)


def _prepare_layers(norm_g, w_in, w_oa, w_ob, w_oc, w_out, b_sink, lam_q1, lam_k1, lam_q2, lam_k2, c_subln_g):
    c_slopes = jnp.asarray(_alibi_slopes(C_HEADS) * np.float32(LOG2E))
    layers = []
    for l in range(DEPTH):
        lam_init = 0.8 - 0.6 * math.exp(-0.3 * l)
        layers.append({
            "w": _layer_weights(w_in[l], w_oa[l], w_ob[l], w_oc[l], w_out[l]),
            "norm_g": norm_g[l].astype(F32).reshape(1, D_MODEL),
            "sink": b_sink[l].astype(F32),
            "scal": jnp.concatenate([c_slopes, jnp.full((1,), lam_init, F32)]),
            "lamv": jnp.stack([lam_q1[l], lam_k1[l], lam_q2[l], lam_k2[l]]).astype(F32),
            "subln_g": c_subln_g[l],
        })
    return layers


def kernel(x_prompt, x_sample, norm_g, w_in, w_oa, w_ob, w_oc, w_out, b_sink, lam_q1, lam_k1, lam_q2, lam_k2, c_subln_g, final_norm_g):
    layers = _prepare_layers(norm_g, w_in, w_oa, w_ob, w_oc, w_out, b_sink,
                             lam_q1, lam_k1, lam_q2, lam_k2, c_subln_g)
    return (_trunk(x_prompt, layers, final_norm_g), _trunk(x_sample, layers, final_norm_g))
```

```python
import functools
import math

import numpy as np
import jax
import jax.numpy as jnp
from jax import lax
from jax.experimental import pallas as pl
from jax.experimental.pallas import tpu as pltpu

F32 = jnp.float32
BF16 = jnp.bfloat16

D_MODEL = 1024
DEPTH = 4
HEAD_DIM = 64
A_PATTERNS = ((128, 1), (512, 4), (2048, 16))
A_GROUPS = 3
A_HEADS = 8
A_WIDTH = A_HEADS * HEAD_DIM
A_HALF = 64
B_HEADS = 8
B_KV_HEADS = 2
B_HALF = 128
B_WIDTH = B_HEADS * HEAD_DIM
C_HEADS = 4
C_VDIM = 2 * HEAD_DIM
C_WIDTH = C_HEADS * C_VDIM
RMS_EPS = 1e-6
NEG_INF = -1e30
IN_SIZES = (
    A_GROUPS * A_WIDTH, A_GROUPS * A_WIDTH, A_GROUPS * A_WIDTH, A_WIDTH,
    B_WIDTH, B_KV_HEADS * HEAD_DIM, B_KV_HEADS * HEAD_DIM, B_WIDTH,
    2 * C_HEADS * HEAD_DIM, 2 * C_HEADS * HEAD_DIM, C_WIDTH, C_WIDTH,
    3 * D_MODEL,
)
IN_OFFSETS = tuple(int(c) for c in np.cumsum((0,) + IN_SIZES))

PROJ_TM = 512
POST_TM = 256
A_TQ = 128
A_STEP = 512
B_TQ = 256
B_UNIT = 256
C_TQ = 256
C_TK = 512
C_UNIT = 256
C_ONES_ROWS = 16
LOG2E = 1.4426950408889634
LANES = 128
VMEM_LIMIT = 56 * 2**20

N_FEAT = B_WIDTH + B_KV_HEADS * HEAD_DIM + 2 * C_HEADS * HEAD_DIM + C_WIDTH


def _alibi_slopes(n):
    return np.asarray([2.0 ** (-8.0 * (i + 1) / n) for i in range(n)], dtype=np.float32)


def _rms(x, g):
    ms = jnp.mean(x * x, axis=-1, keepdims=True)
    return x * lax.rsqrt(ms + RMS_EPS) * g


def _const_spec(shape):
    nd = len(shape)
    return pl.BlockSpec(shape, lambda *_: (0,) * nd, pipeline_mode=pl.Buffered(1))


def _proj_kernel(x_ref, g_ref, wa0_ref, wa1_ref, wa2_ref, wkb_ref, wkc_ref, wft_ref,
                 a0_ref, a1_ref, a2_ref, kb_ref, kc_ref, qbt_ref, vbt_ref, qct_ref, vct_ref, h_scr):
    tm = x_ref.shape[0]
    h32 = _rms(x_ref[...], g_ref[...])
    h = h32.astype(BF16)
    a0_ref[0, 0] = jnp.dot(h, wa0_ref[...], preferred_element_type=F32).astype(BF16)
    n_chunks = h_scr.shape[0]
    for c in range(n_chunks):
        h_scr[c] = h32[:, c * LANES:(c + 1) * LANES]
    for w_ref, o_ref, (_, dil) in ((wa1_ref, a1_ref, A_PATTERNS[1]), (wa2_ref, a2_ref, A_PATTERNS[2])):
        n = tm // dil
        hp = jnp.concatenate(
            [jnp.concatenate([h_scr[c, pl.ds(r, n, stride=dil), :] for c in range(n_chunks)], axis=1)
             for r in range(dil)], axis=0).astype(BF16)
        res = jnp.dot(hp, w_ref[...], preferred_element_type=F32).astype(BF16)
        for r in range(dil):
            o_ref[0, r] = res[r * n:(r + 1) * n]
    for w_ref, o_ref in ((wkb_ref, kb_ref), (wkc_ref, kc_ref)):
        o_ref[...] = jnp.dot(h, w_ref[...], preferred_element_type=F32).astype(BF16)
    ft = lax.dot_general(wft_ref[...], h, (((1,), (1,)), ((), ())),
                         preferred_element_type=F32).astype(BF16)
    r0 = 0
    for o_ref, rows in ((qbt_ref, B_WIDTH), (vbt_ref, B_KV_HEADS * HEAD_DIM),
                        (qct_ref, 2 * C_HEADS * HEAD_DIM), (vct_ref, C_WIDTH)):
        nblk, _, width = o_ref.shape
        for c in range(nblk):
            o_ref[c] = ft[r0:r0 + rows, c * width:(c + 1) * width]
        r0 += rows


def _proj_call(x, g, w, batch, seq):
    t = x.shape[0]
    tm = PROJ_TM
    nt = seq // tm
    tok = lambda n: pl.BlockSpec((tm, n), lambda i: (i, 0))
    feat = lambda rows, width: pl.BlockSpec((tm // width, rows, width), lambda i: (i, 0, 0))
    a_cols = 3 * A_WIDTH
    res = lambda dil: pl.BlockSpec((1, dil, tm // dil, a_cols), lambda i: (i // nt, 0, i % nt, 0))
    a_shape = lambda dil: jax.ShapeDtypeStruct((batch, dil, seq // dil, a_cols), BF16)
    dils = [d for _, d in A_PATTERNS]
    out_shape = (
        a_shape(dils[0]), a_shape(dils[1]), a_shape(dils[2]),
        jax.ShapeDtypeStruct((t, B_KV_HEADS * HEAD_DIM), BF16),
        jax.ShapeDtypeStruct((t, 2 * C_HEADS * HEAD_DIM), BF16),
        jax.ShapeDtypeStruct((t // B_TQ, B_WIDTH, B_TQ), BF16),
        jax.ShapeDtypeStruct((t // B_HALF, B_KV_HEADS * HEAD_DIM, B_HALF), BF16),
        jax.ShapeDtypeStruct((t // C_TQ, 2 * C_HEADS * HEAD_DIM, C_TQ), BF16),
        jax.ShapeDtypeStruct((t // C_TK, C_WIDTH, C_TK), BF16),
    )
    return pl.pallas_call(
        _proj_kernel,
        grid=(t // tm,),
        in_specs=[tok(D_MODEL), _const_spec((1, D_MODEL)),
                  _const_spec(w["a0"].shape), _const_spec(w["a1"].shape), _const_spec(w["a2"].shape),
                  _const_spec(w["kb"].shape), _const_spec(w["kc"].shape), _const_spec(w["ft"].shape)],
        out_specs=(res(dils[0]), res(dils[1]), res(dils[2]), tok(B_KV_HEADS * HEAD_DIM),
                   tok(2 * C_HEADS * HEAD_DIM),
                   feat(B_WIDTH, B_TQ), feat(B_KV_HEADS * HEAD_DIM, B_HALF),
                   feat(2 * C_HEADS * HEAD_DIM, C_TQ), feat(C_WIDTH, C_TK)),
        out_shape=out_shape,
        scratch_shapes=[pltpu.VMEM((D_MODEL // LANES, tm, LANES), F32)],
        compiler_params=pltpu.CompilerParams(dimension_semantics=("parallel",),
                                             vmem_limit_bytes=VMEM_LIMIT),
        name="proj",
    )(x, g, w["a0"], w["a1"], w["a2"], w["kb"], w["kc"], w["ft"])


def _a_bias(gi):
    dil = A_PATTERNS[gi][1]
    slopes = _alibi_slopes(A_GROUPS * A_HEADS).reshape(A_GROUPS, A_HEADS)[gi].astype(np.float64) * LOG2E
    cols = np.arange(A_TQ + 2 * A_HALF)[None, :]
    rel = (cols - A_HALF) - np.arange(A_TQ)[:, None]
    bias = -slopes[:, None, None] * (np.abs(rel) * float(dil))[None]
    out = []
    for variant in range(4):
        ok = np.abs(rel) <= A_HALF
        if variant & 1:
            ok = ok & (cols >= A_HALF)
        if variant & 2:
            ok = ok & (cols < A_TQ + A_HALF)
        out.append(np.where(ok[None], bias, NEG_INF))
    return np.stack(out).astype(np.float32)


def _a_kernel(q_ref, kp_ref, kc_ref, kn_ref, vp_ref, vc_ref, vn_ref, bias_first_ref, bias_mid_ref,
              bias_last_ref, o_ref, lse_ref, kfull, vfull, *, step):
    tkw = A_TQ + 2 * A_HALF
    kfull[0:A_HALF] = kp_ref[0, 0]
    kfull[A_HALF:A_HALF + step] = kc_ref[0, 0]
    kfull[A_HALF + step:] = kn_ref[0, 0]
    vfull[0:A_HALF] = vp_ref[0, 0]
    vfull[A_HALF:A_HALF + step] = vc_ref[0, 0]
    vfull[A_HALF + step:] = vn_ref[0, 0]
    n_sb = step // A_TQ
    lane = lax.broadcasted_iota(jnp.int32, (tkw, 2 * HEAD_DIM), 1)
    lane_o = lax.broadcasted_iota(jnp.int32, (A_TQ, 2 * HEAD_DIM), 1)
    tasks = [(sb, p, hh) for sb in range(n_sb) for p in range(A_HEADS // 2) for hh in range(2)]

    def in_half(hh):
        return (lane >= hh * HEAD_DIM) & (lane < (hh + 1) * HEAD_DIM)

    def scores(task):
        sb, p, hh = task
        r0, c0 = sb * A_TQ, p * 2 * HEAD_DIM
        bias_ref = bias_first_ref if sb == 0 else (bias_last_ref if sb == n_sb - 1 else bias_mid_ref)
        qp = q_ref[0, 0, r0:r0 + A_TQ, c0:c0 + 2 * HEAD_DIM]
        kw = kfull[r0:r0 + tkw, c0:c0 + 2 * HEAD_DIM]
        kz = jnp.where(in_half(hh), kw, jnp.zeros_like(kw))
        s = lax.dot_general(qp, kz, (((1,), (1,)), ((), ())), preferred_element_type=F32)
        s = s + bias_ref[0, 2 * p + hh]
        return s, jnp.max(s, axis=1, keepdims=True)

    def values(task, s, m):
        sb, p, hh = task
        r0, c0 = sb * A_TQ, p * 2 * HEAD_DIM
        vw = vfull[r0:r0 + tkw, c0:c0 + 2 * HEAD_DIM]
        vz = jnp.where(in_half(hh), vw, jnp.zeros_like(vw))
        e = jnp.exp2(s - m)
        l = jnp.sum(e, axis=1, keepdims=True)
        o = jnp.dot(e.astype(BF16), vz, preferred_element_type=F32) / l
        return o, jnp.broadcast_to(m + jnp.log(l) * LOG2E, (A_TQ, 2 * HEAD_DIM))

    nxt = scores(tasks[0])
    o_pair = lse_pair = None
    for ti, task in enumerate(tasks):
        s, m = nxt
        if ti + 1 < len(tasks):
            nxt = scores(tasks[ti + 1])
        o, lse = values(task, s, m)
        sb, p, hh = task
        if hh == 0:
            o_pair, lse_pair = o, lse
        else:
            r0, c0 = sb * A_TQ, p * 2 * HEAD_DIM
            o_ref[0, 0, r0:r0 + A_TQ, c0:c0 + 2 * HEAD_DIM] = o_pair + o
            lse_ref[0, 0, r0:r0 + A_TQ, c0:c0 + 2 * HEAD_DIM] = jnp.where(lane_o < HEAD_DIM, lse_pair, lse)


def _a_call(qkv, gi):
    batch, dil, sub, _ = qkv.shape
    step = min(A_STEP, sub)
    n_steps = sub // step
    hb = step // A_HALF
    nhb = sub // A_HALF
    cur = lambda c: pl.BlockSpec((1, 1, step, A_WIDTH), lambda b, r, u: (b, r, u, c))
    prev = lambda c: pl.BlockSpec((1, 1, A_HALF, A_WIDTH),
                                  lambda b, r, u: (b, r, jnp.maximum(u * hb - 1, 0), c))
    nxt = lambda c: pl.BlockSpec((1, 1, A_HALF, A_WIDTH),
                                 lambda b, r, u: (b, r, jnp.minimum((u + 1) * hb, nhb - 1), c))
    bias = jnp.asarray(_a_bias(gi))
    at_start = lambda u: (u == 0).astype(jnp.int32)
    at_end = lambda u: 2 * (u == n_steps - 1).astype(jnp.int32)
    single = step == A_TQ
    bias_spec = lambda variant: pl.BlockSpec((1,) + bias.shape[1:], lambda b, r, u: (variant(u), 0, 0, 0))
    return pl.pallas_call(
        functools.partial(_a_kernel, step=step),
        grid=(batch, dil, n_steps),
        in_specs=[cur(0), prev(1), cur(1), nxt(1), prev(2), cur(2), nxt(2),
                  bias_spec(lambda u: at_start(u) + (at_end(u) if single else 0)),
                  bias_spec(lambda u: 0 * u), bias_spec(at_end)],
        out_specs=(cur(0), cur(0)),
        out_shape=(jax.ShapeDtypeStruct((batch, dil, sub, A_WIDTH), F32),
                   jax.ShapeDtypeStruct((batch, dil, sub, A_WIDTH), F32)),
        scratch_shapes=[pltpu.VMEM((step + 2 * A_HALF, A_WIDTH), BF16),
                        pltpu.VMEM((step + 2 * A_HALF, A_WIDTH), BF16)],
        compiler_params=pltpu.CompilerParams(
            dimension_semantics=("parallel", "parallel", "parallel"), vmem_limit_bytes=VMEM_LIMIT),
        name=f"mixer_a{gi}",
    )(qkv, qkv, qkv, qkv, qkv, qkv, qkv, bias, bias, bias)


def _b_bias():
    slopes = _alibi_slopes(B_HEADS).astype(np.float64) * LOG2E
    rows = np.arange(B_TQ + 2 * B_HALF)[:, None]
    rel = (rows - B_HALF) - np.arange(B_TQ)[None, :]
    bias = -slopes[:, None, None] * np.abs(rel)[None]
    out = []
    for variant in range(4):
        ok = np.abs(rel) <= B_HALF
        if variant & 1:
            ok = ok & (rows >= B_HALF)
        if variant & 2:
            ok = ok & (rows < B_HALF + B_TQ)
        out.append(np.where(ok[None], bias, NEG_INF))
    return np.stack(out).astype(np.float32)


def _b_kernel(sink_ref, q_ref, kp_ref, kc_ref, kn_ref, vp_ref, vc_ref, vn_ref, bias_ref,
              o_ref, kfull, vfull, s_even, s_odd):
    tk = B_TQ + 2 * B_HALF
    n_units = tk // B_UNIT
    kfull[0:B_HALF] = kp_ref[...]
    kfull[B_HALF:B_HALF + B_TQ] = kc_ref[...]
    kfull[B_HALF + B_TQ:] = kn_ref[...]
    vfull[:, 0:B_HALF] = vp_ref[0]
    for c in range(B_TQ // B_HALF):
        vfull[:, (c + 1) * B_HALF:(c + 2) * B_HALF] = vc_ref[c]
    vfull[:, B_HALF + B_TQ:] = vn_ref[0]
    zeros = jnp.zeros((HEAD_DIM, B_TQ), BF16)
    ones_rows = (lax.broadcasted_iota(jnp.int32, (C_ONES_ROWS, B_UNIT), 0) == 0).astype(BF16)
    neg = jnp.full((1, B_TQ), NEG_INF, F32)
    grp = B_HEADS // B_KV_HEADS

    def score_unit(h, u, s_scr, bm):
        r0 = u * B_UNIT
        qh = q_ref[0, h * HEAD_DIM:(h + 1) * HEAD_DIM, :]
        rhs = jnp.concatenate([qh, zeros] if h // grp == 0 else [zeros, qh], axis=0)
        s = (jnp.dot(kfull[r0:r0 + B_UNIT, :], rhs, preferred_element_type=F32)
             + bias_ref[0, h, r0:r0 + B_UNIT, :])
        s_scr[r0:r0 + B_UNIT, :] = s
        return jnp.maximum(bm, jnp.max(s, axis=0, keepdims=True))

    def value_unit(h, u, s_scr, m, pv):
        r0 = u * B_UNIT
        kvh = h // grp
        v = jnp.concatenate([vfull[kvh * HEAD_DIM:(kvh + 1) * HEAD_DIM, r0:r0 + B_UNIT], ones_rows],
                            axis=0)
        e = jnp.exp2(s_scr[r0:r0 + B_UNIT, :] - m)
        out = jnp.dot(v, e.astype(BF16), preferred_element_type=F32)
        return out if pv is None else pv + out

    scr = (s_even, s_odd)
    bm = neg
    for u in range(n_units):
        bm = score_unit(0, u, s_even, bm)
    outs = []
    for h in range(B_HEADS):
        sink = sink_ref[h]
        m = jnp.maximum(bm, sink)
        bm = neg
        pv = None
        for u in range(n_units):
            if h + 1 < B_HEADS:
                bm = score_unit(h + 1, u, scr[(h + 1) % 2], bm)
            pv = value_unit(h, u, scr[h % 2], m, pv)
        l = pv[HEAD_DIM:HEAD_DIM + 1] + jnp.exp2(sink - m)
        outs.append(pv[:HEAD_DIM] / l)
    o_ref[...] = jnp.concatenate(outs, axis=0).T


def _b_call(sink2, qbt, kb, vbt, batch, seq):
    t = batch * seq
    n_q = seq // B_TQ
    nkb = seq // B_HALF
    r = B_TQ // B_HALF
    kvw = B_KV_HEADS * HEAD_DIM
    tk = B_TQ + 2 * B_HALF
    prev_i = lambda b, i: b * nkb + jnp.maximum(i * r - 1, 0)
    next_i = lambda b, i: b * nkb + jnp.minimum((i + 1) * r, nkb - 1)
    variant = lambda b, i: ((i == 0).astype(jnp.int32) + 2 * (i == n_q - 1).astype(jnp.int32), 0, 0, 0)
    bias = jnp.asarray(_b_bias())
    return pl.pallas_call(
        _b_kernel,
        grid=(batch, n_q),
        in_specs=[pl.BlockSpec(memory_space=pltpu.SMEM),
                  pl.BlockSpec((1, B_WIDTH, B_TQ), lambda b, i: (b * n_q + i, 0, 0)),
                  pl.BlockSpec((B_HALF, kvw), lambda b, i: (prev_i(b, i), 0)),
                  pl.BlockSpec((B_TQ, kvw), lambda b, i: (b * n_q + i, 0)),
                  pl.BlockSpec((B_HALF, kvw), lambda b, i: (next_i(b, i), 0)),
                  pl.BlockSpec((1, kvw, B_HALF), lambda b, i: (prev_i(b, i), 0, 0)),
                  pl.BlockSpec((r, kvw, B_HALF), lambda b, i: (b * n_q + i, 0, 0)),
                  pl.BlockSpec((1, kvw, B_HALF), lambda b, i: (next_i(b, i), 0, 0)),
                  pl.BlockSpec((1,) + bias.shape[1:], variant)],
        out_specs=pl.BlockSpec((B_TQ, B_WIDTH), lambda b, i: (b * n_q + i, 0)),
        out_shape=jax.ShapeDtypeStruct((t, B_WIDTH), F32),
        scratch_shapes=[pltpu.VMEM((tk, kvw), BF16), pltpu.VMEM((kvw, tk), BF16),
                        pltpu.VMEM((tk, B_TQ), F32), pltpu.VMEM((tk, B_TQ), F32)],
        compiler_params=pltpu.CompilerParams(dimension_semantics=("parallel", "parallel"),
                                             vmem_limit_bytes=VMEM_LIMIT),
        name="mixer_b",
    )(sink2, qbt, kb, kb, kb, vbt, vbt, vbt, bias)


def _c_bias():
    slopes = _alibi_slopes(C_HEADS).astype(np.float64) * LOG2E
    d = (np.arange(C_TK)[:, None] - np.arange(C_TQ)[None, :]).astype(np.float64)
    tiles = [d, -d]
    for part in range(C_TK // C_TQ):
        tiles.append(-np.abs(d - part * C_TQ))
    return (slopes[:, None, None, None] * np.stack(tiles)[None]).astype(np.float32)


def _c_kernel(scal_ref, lamv_ref, q_ref, k_ref, v_ref, bias_ref, g_ref, o_ref,
              s_even, s_odd, acc1, acc2, *, n_kv):
    h = pl.program_id(1)
    i = pl.program_id(2)
    ratio = C_TK // C_TQ
    n_units = C_TK // C_UNIT
    jd = i // ratio
    slope2 = scal_ref[h]
    lam_init = scal_ref[C_HEADS]
    q = q_ref[0]
    zeros = jnp.zeros((HEAD_DIM, C_TQ), BF16)
    rhs = (jnp.concatenate([q[:HEAD_DIM], zeros], axis=0),
           jnp.concatenate([zeros, q[HEAD_DIM:]], axis=0))
    accs = (acc1, acc2)
    ones_rows = (lax.broadcasted_iota(jnp.int32, (C_ONES_ROWS, C_UNIT), 0) == 0).astype(BF16)
    acc1[...] = jnp.zeros_like(acc1)
    acc2[...] = jnp.zeros_like(acc2)
    neg = jnp.full((1, C_TQ), NEG_INF, F32)

    def bias_tile(j):
        return jnp.where(j < jd, 0, jnp.where(j > jd, 1, 2 + i % ratio))

    def block_offset(j):
        return jnp.where(j == jd, 0.0, -slope2 * jnp.abs(C_TQ * i - C_TK * j).astype(F32))

    def score_unit(j, tile, u, s_scr, bm):
        r0 = u * C_UNIT
        k = k_ref[pl.ds(pl.multiple_of(j * C_TK + r0, C_UNIT), C_UNIT), :]
        bias = bias_ref[0, tile, r0:r0 + C_UNIT, :]
        out = []
        for mp in range(2):
            s = jnp.dot(k, rhs[mp], preferred_element_type=F32) + bias
            s_scr[mp, r0:r0 + C_UNIT, :] = s
            out.append(jnp.maximum(bm[mp], jnp.max(s, axis=0, keepdims=True)))
        return tuple(out)

    def value_unit(j, u, s_scr, mn, pvs):
        r0 = u * C_UNIT
        v = jnp.concatenate([v_ref[j, :, r0:r0 + C_UNIT], ones_rows], axis=0)
        out = []
        for mp in range(2):
            e = jnp.exp2(s_scr[mp, r0:r0 + C_UNIT, :] - mn[mp])
            pv = jnp.dot(v, e.astype(BF16), preferred_element_type=F32)
            out.append(pv if pvs[mp] is None else pvs[mp] + pv)
        return tuple(out)

    def stage(j_cur, s_cur, bm_cur, m, j_next, s_next):
        c = block_offset(j_cur)
        mn, alpha, m_new = [], [], []
        for mp in range(2):
            msh = m[mp] - c
            x = jnp.maximum(msh, bm_cur[mp])
            mn.append(x)
            alpha.append(jnp.exp2(msh - x))
            m_new.append(x + c)
        bm_next = (neg, neg)
        pvs = (None, None)
        for u in range(n_units):
            if j_next is not None:
                bm_next = score_unit(j_next, bias_tile(j_next), u, s_next, bm_next)
            pvs = value_unit(j_cur, u, s_cur, mn, pvs)
        for mp in range(2):
            accs[mp][...] = alpha[mp] * accs[mp][...] + pvs[mp]
        return bm_next, tuple(m_new)

    bm = (neg, neg)
    for u in range(n_units):
        bm = score_unit(0, bias_tile(0), u, s_even, bm)

    def pair(jj, carry):
        bm, m = carry
        j = 2 * jj
        bm, m = stage(j, s_even, bm, m, j + 1, s_odd)
        return stage(j + 1, s_odd, bm, m, j + 2, s_even)

    bm, m = lax.fori_loop(0, n_kv // 2 - 1, pair, (bm, (neg, neg)))
    bm, m = stage(n_kv - 2, s_even, bm, m, n_kv - 1, s_odd)
    stage(n_kv - 1, s_odd, bm, m, None, None)

    lv = lamv_ref[...]
    lam = (jnp.exp(jnp.sum(lv[0:1] * lv[1:2], axis=1, keepdims=True))
           - jnp.exp(jnp.sum(lv[2:3] * lv[3:4], axis=1, keepdims=True)) + lam_init)
    a1 = acc1[...]
    a2 = acc2[...]
    a = a1[:C_VDIM] / a1[C_VDIM:C_VDIM + 1] - lam * (a2[:C_VDIM] / a2[C_VDIM:C_VDIM + 1])
    ms = jnp.mean(a * a, axis=0, keepdims=True)
    y = a * lax.rsqrt(ms + RMS_EPS) * g_ref[...] * (1.0 - lam_init)
    o_ref[...] = y.T


def _c_call(scal, lamv, subln_g, qct, kc, vct, batch, seq):
    t = batch * seq
    n_q = seq // C_TQ
    n_kv = seq // C_TK
    kw = 2 * HEAD_DIM
    bias = jnp.asarray(_c_bias())
    g = jnp.broadcast_to(subln_g.astype(F32)[:, None], (C_VDIM, C_TQ))
    return pl.pallas_call(
        functools.partial(_c_kernel, n_kv=n_kv),
        grid=(batch, C_HEADS, n_q),
        in_specs=[pl.BlockSpec(memory_space=pltpu.SMEM),
                  _const_spec((4, HEAD_DIM)),
                  pl.BlockSpec((1, kw, C_TQ), lambda b, h, i: (b * n_q + i, h, 0)),
                  pl.BlockSpec((seq, kw), lambda b, h, i: (b, h)),
                  pl.BlockSpec((n_kv, C_VDIM, C_TK), lambda b, h, i: (b, h, 0)),
                  pl.BlockSpec((1,) + bias.shape[1:], lambda b, h, i: (h, 0, 0, 0)),
                  _const_spec((C_VDIM, C_TQ))],
        out_specs=pl.BlockSpec((C_TQ, C_VDIM), lambda b, h, i: (b * n_q + i, h)),
        out_shape=jax.ShapeDtypeStruct((t, C_WIDTH), F32),
        scratch_shapes=[pltpu.VMEM((2, C_TK, C_TQ), F32), pltpu.VMEM((2, C_TK, C_TQ), F32),
                        pltpu.VMEM((C_VDIM + C_ONES_ROWS, C_TQ), F32),
                        pltpu.VMEM((C_VDIM + C_ONES_ROWS, C_TQ), F32)],
        compiler_params=pltpu.CompilerParams(
            dimension_semantics=("parallel", "parallel", "parallel"), vmem_limit_bytes=VMEM_LIMIT),
        name="mixer_c",
    )(scal, lamv, qct, kc, vct, bias, g)


def _sigmoid(x):
    return 1.0 / (1.0 + jnp.exp(-x))


def _post_kernel(x_ref, g_ref, oa0_ref, oa1_ref, oa2_ref, la0_ref, la1_ref, la2_ref, ob_ref, oc_ref,
                 wg_ref, woa_ref, wob_ref, woc_ref, wout_ref, fg_ref, y_ref, tok_scr, *, final):
    x = x_ref[...]
    h = _rms(x, g_ref[...]).astype(BF16)

    def token_order(ref, slot):
        _, dil, n, width = ref.shape
        if dil == 1:
            return ref[0, 0]
        n_chunks = width // LANES
        for r in range(dil):
            rows = ref[0, r]
            for c in range(n_chunks):
                tok_scr[slot, c, pl.ds(r, n, stride=dil), :] = rows[:, c * LANES:(c + 1) * LANES]
        return jnp.concatenate([tok_scr[slot, c] for c in range(n_chunks)], axis=1)

    def gate(c0, n):
        return jnp.dot(h, wg_ref[:, c0:c0 + n], preferred_element_type=F32)

    def silu_gated(o, c0):
        ga = gate(c0, o.shape[1])
        return (o * (ga * _sigmoid(ga))).astype(BF16)

    l0, l1, l2 = token_order(la0_ref, 0), token_order(la1_ref, 0), token_order(la2_ref, 1)
    o0, o1, o2 = token_order(oa0_ref, 0), token_order(oa1_ref, 2), token_order(oa2_ref, 3)
    m = jnp.maximum(jnp.maximum(l0, l1), l2)
    e0, e1, e2 = jnp.exp2(l0 - m), jnp.exp2(l1 - m), jnp.exp2(l2 - m)
    oa = (e0 * o0 + e1 * o1 + e2 * o2) / (e0 + e1 + e2)

    ya = jnp.dot(silu_gated(oa, 0), woa_ref[...], preferred_element_type=F32)
    yb = jnp.dot(silu_gated(ob_ref[...], A_WIDTH), wob_ref[...], preferred_element_type=F32)
    yc = jnp.dot(silu_gated(oc_ref[...], A_WIDTH + B_WIDTH), woc_ref[...], preferred_element_type=F32)
    g0 = A_WIDTH + B_WIDTH + C_WIDTH
    mixed = (_sigmoid(gate(g0, D_MODEL)) * ya + _sigmoid(gate(g0 + D_MODEL, D_MODEL)) * yb
             + _sigmoid(gate(g0 + 2 * D_MODEL, D_MODEL)) * yc)
    y = x + jnp.dot(mixed.astype(BF16), wout_ref[...], preferred_element_type=F32)
    if final:
        y = _rms(y, fg_ref[...])
    y_ref[...] = y


def _post_call(x, g, oa, la, ob, oc, w, final_g, final, seq):
    t = x.shape[0]
    tm = POST_TM
    nt = seq // tm
    tok = lambda n: pl.BlockSpec((tm, n), lambda i: (i, 0))
    res = [pl.BlockSpec((1, dil, tm // dil, A_WIDTH), lambda i: (i // nt, 0, i % nt, 0))
           for _, dil in A_PATTERNS]
    return pl.pallas_call(
        functools.partial(_post_kernel, final=final),
        grid=(t // tm,),
        in_specs=[tok(D_MODEL), _const_spec((1, D_MODEL))] + res + res + [tok(B_WIDTH), tok(C_WIDTH)]
                 + [_const_spec(w["g"].shape), _const_spec(w["oa"].shape), _const_spec(w["ob"].shape),
                    _const_spec(w["oc"].shape), _const_spec(w["out"].shape), _const_spec((1, D_MODEL))],
        out_specs=tok(D_MODEL),
        out_shape=jax.ShapeDtypeStruct((t, D_MODEL), F32),
        scratch_shapes=[pltpu.VMEM((4, A_WIDTH // LANES, tm, LANES), F32)],
        compiler_params=pltpu.CompilerParams(dimension_semantics=("parallel",),
                                             vmem_limit_bytes=VMEM_LIMIT),
        name="post_final" if final else "post",
    )(x, g, oa[0], oa[1], oa[2], la[0], la[1], la[2], ob, oc,
      w["g"], w["oa"], w["ob"], w["oc"], w["out"], final_g)


def _layer_weights(w_in, w_oa, w_ob, w_oc, w_out):
    col = lambda idx: w_in[:, IN_OFFSETS[idx]:IN_OFFSETS[idx + 1]]
    scale = HEAD_DIM ** -0.5
    ft = jnp.concatenate([col(4) * (scale * LOG2E), col(6), col(8) * (scale * LOG2E), col(10)], axis=1).T
    grp = lambda idx, gi: col(idx)[:, gi * A_WIDTH:(gi + 1) * A_WIDTH]
    a_qkv = lambda gi: jnp.concatenate([grp(0, gi) * (scale * LOG2E), grp(1, gi), grp(2, gi)],
                                       axis=1).astype(BF16)
    return {
        "a0": a_qkv(0), "a1": a_qkv(1), "a2": a_qkv(2),
        "kb": col(5).astype(BF16), "kc": col(9).astype(BF16), "ft": ft.astype(BF16),
        "g": jnp.concatenate([col(3), col(7), col(11), col(12)], axis=1).astype(BF16),
        "oa": w_oa.astype(BF16), "ob": w_ob.astype(BF16), "oc": w_oc.astype(BF16),
        "out": w_out.astype(BF16),
    }


def _trunk(x3, layers, final_g):
    batch, seq, _ = x3.shape
    x = x3.reshape(batch * seq, D_MODEL)
    fg = final_g.astype(F32).reshape(1, D_MODEL)
    for li, lw in enumerate(layers):
        w = lw["w"]
        a0, a1, a2, kb, kc, qbt, vbt, qct, vct = _proj_call(x, lw["norm_g"], w, batch, seq)
        oa, la = zip(*[_a_call(qkv, gi) for gi, qkv in enumerate((a0, a1, a2))])
        ob = _b_call(lw["sink"], qbt, kb, vbt, batch, seq)
        oc = _c_call(lw["scal"], lw["lamv"], lw["subln_g"], qct, kc, vct, batch, seq)
        x = _post_call(x, lw["norm_g"], oa, la, ob, oc, w, fg, final=(li == len(layers) - 1), seq=seq)
    return x.reshape(batch, seq, D_MODEL)


def _prepare_layers(norm_g, w_in, w_oa, w_ob, w_oc, w_out, b_sink, lam_q1, lam_k1, lam_q2, lam_k2, c_subln_g):
    c_slopes = jnp.asarray(_alibi_slopes(C_HEADS) * np.float32(LOG2E))
    layers = []
    for l in range(DEPTH):
        lam_init = 0.8 - 0.6 * math.exp(-0.3 * l)
        layers.append({
            "w": _layer_weights(w_in[l], w_oa[l], w_ob[l], w_oc[l], w_out[l]),
            "norm_g": norm_g[l].astype(F32).reshape(1, D_MODEL),
            "sink": b_sink[l].astype(F32) * LOG2E,
            "scal": jnp.concatenate([c_slopes, jnp.full((1,), lam_init, F32)]),
            "lamv": jnp.stack([lam_q1[l], lam_k1[l], lam_q2[l], lam_k2[l]]).astype(F32),
            "subln_g": c_subln_g[l],
        })
    return layers


def kernel(x_prompt, x_sample, norm_g, w_in, w_oa, w_ob, w_oc, w_out, b_sink, lam_q1, lam_k1, lam_q2, lam_k2, c_subln_g, final_norm_g):
    layers = _prepare_layers(norm_g, w_in, w_oa, w_ob, w_oc, w_out, b_sink,
                             lam_q1, lam_k1, lam_q2, lam_k2, c_subln_g)
    return (_trunk(x_prompt, layers, final_norm_g), _trunk(x_sample, layers, final_norm_g))
```

```python
import functools
import math

import numpy as np
import jax
import jax.numpy as jnp
from jax import lax
from jax.experimental import pallas as pl
from jax.experimental.pallas import tpu as pltpu

F32 = jnp.float32
BF16 = jnp.bfloat16

D_MODEL = 1024
DEPTH = 4
HEAD_DIM = 64
A_PATTERNS = ((128, 1), (512, 4), (2048, 16))
A_GROUPS = 3
A_HEADS = 8
A_WIDTH = A_HEADS * HEAD_DIM
A_HALF = 64
B_HEADS = 8
B_KV_HEADS = 2
B_HALF = 128
B_WIDTH = B_HEADS * HEAD_DIM
C_HEADS = 4
C_VDIM = 2 * HEAD_DIM
C_WIDTH = C_HEADS * C_VDIM
RMS_EPS = 1e-6
NEG_INF = -1e30
IN_SIZES = (
    A_GROUPS * A_WIDTH, A_GROUPS * A_WIDTH, A_GROUPS * A_WIDTH, A_WIDTH,
    B_WIDTH, B_KV_HEADS * HEAD_DIM, B_KV_HEADS * HEAD_DIM, B_WIDTH,
    2 * C_HEADS * HEAD_DIM, 2 * C_HEADS * HEAD_DIM, C_WIDTH, C_WIDTH,
    3 * D_MODEL,
)
IN_OFFSETS = tuple(int(c) for c in np.cumsum((0,) + IN_SIZES))

PROJ_TM = 512
POST_TM = 256
A_TQ = 128
A_STEP = 512
B_TQ = 256
B_UNIT = 256
C_TQ = 256
C_TK = 512
C_UNIT = 256
C_STAGES = 4
C_ONES_ROWS = 16
LOG2E = 1.4426950408889634
LANES = 128
VMEM_LIMIT = 56 * 2**20

N_FEAT = B_WIDTH + B_KV_HEADS * HEAD_DIM + 2 * C_HEADS * HEAD_DIM + C_WIDTH


def _alibi_slopes(n):
    return np.asarray([2.0 ** (-8.0 * (i + 1) / n) for i in range(n)], dtype=np.float32)


def _rms(x, g):
    ms = jnp.mean(x * x, axis=-1, keepdims=True)
    return x * lax.rsqrt(ms + RMS_EPS) * g


def _const_spec(shape):
    nd = len(shape)
    return pl.BlockSpec(shape, lambda *_: (0,) * nd, pipeline_mode=pl.Buffered(1))


def _proj_kernel(x_ref, g_ref, wa0_ref, wa1_ref, wa2_ref, wkb_ref, wkc_ref, wft_ref,
                 a0_ref, a1_ref, a2_ref, kb_ref, kc_ref, qbt_ref, vbt_ref, qct_ref, vct_ref, h_scr):
    tm = x_ref.shape[0]
    h32 = _rms(x_ref[...], g_ref[...])
    h = h32.astype(BF16)
    a0_ref[0, 0] = jnp.dot(h, wa0_ref[...], preferred_element_type=F32).astype(BF16)
    n_chunks = h_scr.shape[0]
    for c in range(n_chunks):
        h_scr[c] = h32[:, c * LANES:(c + 1) * LANES]
    for w_ref, o_ref, (_, dil) in ((wa1_ref, a1_ref, A_PATTERNS[1]), (wa2_ref, a2_ref, A_PATTERNS[2])):
        n = tm // dil
        hp = jnp.concatenate(
            [jnp.concatenate([h_scr[c, pl.ds(r, n, stride=dil), :] for c in range(n_chunks)], axis=1)
             for r in range(dil)], axis=0).astype(BF16)
        res = jnp.dot(hp, w_ref[...], preferred_element_type=F32).astype(BF16)
        for r in range(dil):
            o_ref[0, r] = res[r * n:(r + 1) * n]
    for w_ref, o_ref in ((wkb_ref, kb_ref), (wkc_ref, kc_ref)):
        o_ref[...] = jnp.dot(h, w_ref[...], preferred_element_type=F32).astype(BF16)
    ft = lax.dot_general(wft_ref[...], h, (((1,), (1,)), ((), ())),
                         preferred_element_type=F32).astype(BF16)
    r0 = 0
    for o_ref, rows in ((qbt_ref, B_WIDTH), (vbt_ref, B_KV_HEADS * HEAD_DIM),
                        (qct_ref, 2 * C_HEADS * HEAD_DIM), (vct_ref, C_WIDTH)):
        nblk, _, width = o_ref.shape
        for c in range(nblk):
            o_ref[c] = ft[r0:r0 + rows, c * width:(c + 1) * width]
        r0 += rows


def _proj_call(x, g, w, batch, seq):
    t = x.shape[0]
    tm = PROJ_TM
    nt = seq // tm
    tok = lambda n: pl.BlockSpec((tm, n), lambda i: (i, 0))
    feat = lambda rows, width: pl.BlockSpec((tm // width, rows, width), lambda i: (i, 0, 0))
    a_cols = 3 * A_WIDTH
    res = lambda dil: pl.BlockSpec((1, dil, tm // dil, a_cols), lambda i: (i // nt, 0, i % nt, 0))
    a_shape = lambda dil: jax.ShapeDtypeStruct((batch, dil, seq // dil, a_cols), BF16)
    dils = [d for _, d in A_PATTERNS]
    out_shape = (
        a_shape(dils[0]), a_shape(dils[1]), a_shape(dils[2]),
        jax.ShapeDtypeStruct((t, B_KV_HEADS * HEAD_DIM), BF16),
        jax.ShapeDtypeStruct((t, 2 * C_HEADS * HEAD_DIM), BF16),
        jax.ShapeDtypeStruct((t // B_TQ, B_WIDTH, B_TQ), BF16),
        jax.ShapeDtypeStruct((t // B_HALF, B_KV_HEADS * HEAD_DIM, B_HALF), BF16),
        jax.ShapeDtypeStruct((t // C_TQ, 2 * C_HEADS * HEAD_DIM, C_TQ), BF16),
        jax.ShapeDtypeStruct((t // C_TK, C_WIDTH, C_TK), BF16),
    )
    return pl.pallas_call(
        _proj_kernel,
        grid=(t // tm,),
        in_specs=[tok(D_MODEL), _const_spec((1, D_MODEL)),
                  _const_spec(w["a0"].shape), _const_spec(w["a1"].shape), _const_spec(w["a2"].shape),
                  _const_spec(w["kb"].shape), _const_spec(w["kc"].shape), _const_spec(w["ft"].shape)],
        out_specs=(res(dils[0]), res(dils[1]), res(dils[2]), tok(B_KV_HEADS * HEAD_DIM),
                   tok(2 * C_HEADS * HEAD_DIM),
                   feat(B_WIDTH, B_TQ), feat(B_KV_HEADS * HEAD_DIM, B_HALF),
                   feat(2 * C_HEADS * HEAD_DIM, C_TQ), feat(C_WIDTH, C_TK)),
        out_shape=out_shape,
        scratch_shapes=[pltpu.VMEM((D_MODEL // LANES, tm, LANES), F32)],
        compiler_params=pltpu.CompilerParams(dimension_semantics=("parallel",),
                                             vmem_limit_bytes=VMEM_LIMIT),
        name="proj",
    )(x, g, w["a0"], w["a1"], w["a2"], w["kb"], w["kc"], w["ft"])


def _a_bias(gi):
    dil = A_PATTERNS[gi][1]
    slopes = _alibi_slopes(A_GROUPS * A_HEADS).reshape(A_GROUPS, A_HEADS)[gi].astype(np.float64) * LOG2E
    cols = np.arange(A_TQ + 2 * A_HALF)[None, :]
    rel = (cols - A_HALF) - np.arange(A_TQ)[:, None]
    bias = -slopes[:, None, None] * (np.abs(rel) * float(dil))[None]
    out = []
    for variant in range(4):
        ok = np.abs(rel) <= A_HALF
        if variant & 1:
            ok = ok & (cols >= A_HALF)
        if variant & 2:
            ok = ok & (cols < A_TQ + A_HALF)
        out.append(np.where(ok[None], bias, NEG_INF))
    return np.stack(out).astype(np.float32)


def _a_kernel(q_ref, kp_ref, kc_ref, kn_ref, vp_ref, vc_ref, vn_ref, bias_first_ref, bias_mid_ref,
              bias_last_ref, o_ref, lse_ref, kfull, vfull, *, step):
    tkw = A_TQ + 2 * A_HALF
    kfull[0:A_HALF] = kp_ref[0, 0]
    kfull[A_HALF:A_HALF + step] = kc_ref[0, 0]
    kfull[A_HALF + step:] = kn_ref[0, 0]
    vfull[0:A_HALF] = vp_ref[0, 0]
    vfull[A_HALF:A_HALF + step] = vc_ref[0, 0]
    vfull[A_HALF + step:] = vn_ref[0, 0]
    n_sb = step // A_TQ
    lane = lax.broadcasted_iota(jnp.int32, (tkw, 2 * HEAD_DIM), 1)
    lane_o = lax.broadcasted_iota(jnp.int32, (A_TQ, 2 * HEAD_DIM), 1)
    tasks = [(sb, p, hh) for sb in range(n_sb) for p in range(A_HEADS // 2) for hh in range(2)]

    def in_half(hh):
        return (lane >= hh * HEAD_DIM) & (lane < (hh + 1) * HEAD_DIM)

    def scores(task):
        sb, p, hh = task
        r0, c0 = sb * A_TQ, p * 2 * HEAD_DIM
        bias_ref = bias_first_ref if sb == 0 else (bias_last_ref if sb == n_sb - 1 else bias_mid_ref)
        qp = q_ref[0, 0, r0:r0 + A_TQ, c0:c0 + 2 * HEAD_DIM]
        kw = kfull[r0:r0 + tkw, c0:c0 + 2 * HEAD_DIM]
        kz = jnp.where(in_half(hh), kw, jnp.zeros_like(kw))
        s = lax.dot_general(qp, kz, (((1,), (1,)), ((), ())), preferred_element_type=F32)
        s = s + bias_ref[0, 2 * p + hh]
        return s, jnp.max(s, axis=1, keepdims=True)

    def values(task, s, m):
        sb, p, hh = task
        r0, c0 = sb * A_TQ, p * 2 * HEAD_DIM
        vw = vfull[r0:r0 + tkw, c0:c0 + 2 * HEAD_DIM]
        vz = jnp.where(in_half(hh), vw, jnp.zeros_like(vw))
        e = jnp.exp2(s - m)
        l = jnp.sum(e, axis=1, keepdims=True)
        o = jnp.dot(e.astype(BF16), vz, preferred_element_type=F32) / l
        return o, jnp.broadcast_to(m + jnp.log(l) * LOG2E, (A_TQ, 2 * HEAD_DIM))

    nxt = scores(tasks[0])
    o_pair = lse_pair = None
    for ti, task in enumerate(tasks):
        s, m = nxt
        if ti + 1 < len(tasks):
            nxt = scores(tasks[ti + 1])
        o, lse = values(task, s, m)
        sb, p, hh = task
        if hh == 0:
            o_pair, lse_pair = o, lse
        else:
            r0, c0 = sb * A_TQ, p * 2 * HEAD_DIM
            o_ref[0, 0, r0:r0 + A_TQ, c0:c0 + 2 * HEAD_DIM] = (o_pair + o).astype(o_ref.dtype)
            lse_ref[0, 0, r0:r0 + A_TQ, c0:c0 + 2 * HEAD_DIM] = jnp.where(lane_o < HEAD_DIM, lse_pair, lse)


def _a_call(qkv, gi):
    batch, dil, sub, _ = qkv.shape
    step = min(A_STEP, sub)
    n_steps = sub // step
    hb = step // A_HALF
    nhb = sub // A_HALF
    cur = lambda c: pl.BlockSpec((1, 1, step, A_WIDTH), lambda b, r, u: (b, r, u, c))
    prev = lambda c: pl.BlockSpec((1, 1, A_HALF, A_WIDTH),
                                  lambda b, r, u: (b, r, jnp.maximum(u * hb - 1, 0), c))
    nxt = lambda c: pl.BlockSpec((1, 1, A_HALF, A_WIDTH),
                                 lambda b, r, u: (b, r, jnp.minimum((u + 1) * hb, nhb - 1), c))
    bias = jnp.asarray(_a_bias(gi))
    at_start = lambda u: (u == 0).astype(jnp.int32)
    at_end = lambda u: 2 * (u == n_steps - 1).astype(jnp.int32)
    single = step == A_TQ
    bias_spec = lambda variant: pl.BlockSpec((1,) + bias.shape[1:], lambda b, r, u: (variant(u), 0, 0, 0))
    return pl.pallas_call(
        functools.partial(_a_kernel, step=step),
        grid=(batch, dil, n_steps),
        in_specs=[cur(0), prev(1), cur(1), nxt(1), prev(2), cur(2), nxt(2),
                  bias_spec(lambda u: at_start(u) + (at_end(u) if single else 0)),
                  bias_spec(lambda u: 0 * u), bias_spec(at_end)],
        out_specs=(cur(0), cur(0)),
        out_shape=(jax.ShapeDtypeStruct((batch, dil, sub, A_WIDTH), BF16),
                   jax.ShapeDtypeStruct((batch, dil, sub, A_WIDTH), F32)),
        scratch_shapes=[pltpu.VMEM((step + 2 * A_HALF, A_WIDTH), BF16),
                        pltpu.VMEM((step + 2 * A_HALF, A_WIDTH), BF16)],
        compiler_params=pltpu.CompilerParams(
            dimension_semantics=("parallel", "parallel", "parallel"), vmem_limit_bytes=VMEM_LIMIT),
        name=f"mixer_a{gi}",
    )(qkv, qkv, qkv, qkv, qkv, qkv, qkv, bias, bias, bias)


def _b_bias():
    slopes = _alibi_slopes(B_HEADS).astype(np.float64) * LOG2E
    rows = np.arange(B_TQ + 2 * B_HALF)[:, None]
    rel = (rows - B_HALF) - np.arange(B_TQ)[None, :]
    bias = -slopes[:, None, None] * np.abs(rel)[None]
    out = []
    for variant in range(4):
        ok = np.abs(rel) <= B_HALF
        if variant & 1:
            ok = ok & (rows >= B_HALF)
        if variant & 2:
            ok = ok & (rows < B_HALF + B_TQ)
        out.append(np.where(ok[None], bias, NEG_INF))
    return np.stack(out).astype(np.float32)


def _b_kernel(sink_ref, q_ref, kp_ref, kc_ref, kn_ref, vp_ref, vc_ref, vn_ref, bias_ref,
              o_ref, kfull, vfull, s_even, s_odd):
    tk = B_TQ + 2 * B_HALF
    n_units = tk // B_UNIT
    kfull[0:B_HALF] = kp_ref[...]
    kfull[B_HALF:B_HALF + B_TQ] = kc_ref[...]
    kfull[B_HALF + B_TQ:] = kn_ref[...]
    vfull[:, 0:B_HALF] = vp_ref[0]
    for c in range(B_TQ // B_HALF):
        vfull[:, (c + 1) * B_HALF:(c + 2) * B_HALF] = vc_ref[c]
    vfull[:, B_HALF + B_TQ:] = vn_ref[0]
    zeros = jnp.zeros((HEAD_DIM, B_TQ), BF16)
    ones_rows = (lax.broadcasted_iota(jnp.int32, (C_ONES_ROWS, B_UNIT), 0) == 0).astype(BF16)
    neg = jnp.full((1, B_TQ), NEG_INF, F32)
    grp = B_HEADS // B_KV_HEADS

    def score_unit(h, u, s_scr, bm):
        r0 = u * B_UNIT
        qh = q_ref[0, h * HEAD_DIM:(h + 1) * HEAD_DIM, :]
        rhs = jnp.concatenate([qh, zeros] if h // grp == 0 else [zeros, qh], axis=0)
        s = (jnp.dot(kfull[r0:r0 + B_UNIT, :], rhs, preferred_element_type=F32)
             + bias_ref[0, h, r0:r0 + B_UNIT, :])
        s_scr[r0:r0 + B_UNIT, :] = s
        return jnp.maximum(bm, jnp.max(s, axis=0, keepdims=True))

    def value_unit(h, u, s_scr, m, pv):
        r0 = u * B_UNIT
        kvh = h // grp
        v = jnp.concatenate([vfull[kvh * HEAD_DIM:(kvh + 1) * HEAD_DIM, r0:r0 + B_UNIT], ones_rows],
                            axis=0)
        e = jnp.exp2(s_scr[r0:r0 + B_UNIT, :] - m)
        out = jnp.dot(v, e.astype(BF16), preferred_element_type=F32)
        return out if pv is None else pv + out

    scr = (s_even, s_odd)
    bm = neg
    for u in range(n_units):
        bm = score_unit(0, u, s_even, bm)
    outs = []
    for h in range(B_HEADS):
        sink = sink_ref[h]
        m = jnp.maximum(bm, sink)
        bm = neg
        pv = None
        for u in range(n_units):
            if h + 1 < B_HEADS:
                bm = score_unit(h + 1, u, scr[(h + 1) % 2], bm)
            pv = value_unit(h, u, scr[h % 2], m, pv)
        l = pv[HEAD_DIM:HEAD_DIM + 1] + jnp.exp2(sink - m)
        outs.append(pv[:HEAD_DIM] / l)
    o_ref[...] = jnp.concatenate(outs, axis=0).T.astype(o_ref.dtype)


def _b_call(sink2, qbt, kb, vbt, batch, seq):
    t = batch * seq
    n_q = seq // B_TQ
    nkb = seq // B_HALF
    r = B_TQ // B_HALF
    kvw = B_KV_HEADS * HEAD_DIM
    tk = B_TQ + 2 * B_HALF
    prev_i = lambda b, i: b * nkb + jnp.maximum(i * r - 1, 0)
    next_i = lambda b, i: b * nkb + jnp.minimum((i + 1) * r, nkb - 1)
    variant = lambda b, i: ((i == 0).astype(jnp.int32) + 2 * (i == n_q - 1).astype(jnp.int32), 0, 0, 0)
    bias = jnp.asarray(_b_bias())
    return pl.pallas_call(
        _b_kernel,
        grid=(batch, n_q),
        in_specs=[pl.BlockSpec(memory_space=pltpu.SMEM),
                  pl.BlockSpec((1, B_WIDTH, B_TQ), lambda b, i: (b * n_q + i, 0, 0)),
                  pl.BlockSpec((B_HALF, kvw), lambda b, i: (prev_i(b, i), 0)),
                  pl.BlockSpec((B_TQ, kvw), lambda b, i: (b * n_q + i, 0)),
                  pl.BlockSpec((B_HALF, kvw), lambda b, i: (next_i(b, i), 0)),
                  pl.BlockSpec((1, kvw, B_HALF), lambda b, i: (prev_i(b, i), 0, 0)),
                  pl.BlockSpec((r, kvw, B_HALF), lambda b, i: (b * n_q + i, 0, 0)),
                  pl.BlockSpec((1, kvw, B_HALF), lambda b, i: (next_i(b, i), 0, 0)),
                  pl.BlockSpec((1,) + bias.shape[1:], variant)],
        out_specs=pl.BlockSpec((B_TQ, B_WIDTH), lambda b, i: (b * n_q + i, 0)),
        out_shape=jax.ShapeDtypeStruct((t, B_WIDTH), BF16),
        scratch_shapes=[pltpu.VMEM((tk, kvw), BF16), pltpu.VMEM((kvw, tk), BF16),
                        pltpu.VMEM((tk, B_TQ), F32), pltpu.VMEM((tk, B_TQ), F32)],
        compiler_params=pltpu.CompilerParams(dimension_semantics=("parallel", "parallel"),
                                             vmem_limit_bytes=VMEM_LIMIT),
        name="mixer_b",
    )(sink2, qbt, kb, kb, kb, vbt, vbt, vbt, bias)


def _c_bias():
    slopes = _alibi_slopes(C_HEADS).astype(np.float64) * LOG2E
    d = (np.arange(C_TK)[:, None] - np.arange(C_TQ)[None, :]).astype(np.float64)
    tiles = [d, -d]
    for part in range(C_TK // C_TQ):
        tiles.append(-np.abs(d - part * C_TQ))
    return (slopes[:, None, None, None] * np.stack(tiles)[None]).astype(np.float32)


def _c_kernel(scal_ref, lamv_ref, q_ref, k_ref, v_ref, bias_ref, g_ref, o_ref,
              s_even, s_odd, acc1, acc2, *, n_q, n_kv):
    h = pl.program_id(1)
    ratio = C_TK // C_TQ
    n_units = C_TK // C_UNIT
    slope2 = scal_ref[h]
    lam_init = scal_ref[C_HEADS]
    accs = (acc1, acc2)
    zeros = jnp.zeros((HEAD_DIM, C_TQ), BF16)
    ones_rows = (lax.broadcasted_iota(jnp.int32, (C_ONES_ROWS, C_UNIT), 0) == 0).astype(BF16)
    neg = jnp.full((1, C_TQ), NEG_INF, F32)
    lv = lamv_ref[...]
    lam = (jnp.exp(jnp.sum(lv[0:1] * lv[1:2], axis=1, keepdims=True))
           - jnp.exp(jnp.sum(lv[2:3] * lv[3:4], axis=1, keepdims=True)) + lam_init)
    acc1[...] = jnp.zeros_like(acc1)
    acc2[...] = jnp.zeros_like(acc2)

    def block_offset(i, j):
        return jnp.where(j == i // ratio, 0.0, -slope2 * jnp.abs(C_TQ * i - C_TK * j).astype(F32))

    def score_units(i, j, s_scr):
        jd = i // ratio
        tile = jnp.where(j < jd, 0, jnp.where(j > jd, 1, 2 + i % ratio))
        q = q_ref[i]
        rhs = (jnp.concatenate([q[:HEAD_DIM], zeros], axis=0),
               jnp.concatenate([zeros, q[HEAD_DIM:]], axis=0))
        bm = [neg, neg]
        for u in range(n_units):
            r0 = u * C_UNIT
            k = k_ref[pl.ds(pl.multiple_of(j * C_TK + r0, C_UNIT), C_UNIT), :]
            bias = bias_ref[0, tile, r0:r0 + C_UNIT, :]
            for mp in range(2):
                s = jnp.dot(k, rhs[mp], preferred_element_type=F32) + bias
                s_scr[mp, r0:r0 + C_UNIT, :] = s
                bm[mp] = jnp.maximum(bm[mp], jnp.max(s, axis=0, keepdims=True))
            yield tuple(bm)

    def value_unit(j, u, s_scr, mn, pvs):
        r0 = u * C_UNIT
        v = jnp.concatenate([v_ref[j, :, r0:r0 + C_UNIT], ones_rows], axis=0)
        out = []
        for mp in range(2):
            e = jnp.exp2(s_scr[mp, r0:r0 + C_UNIT, :] - mn[mp])
            pv = jnp.dot(v, e.astype(BF16), preferred_element_type=F32)
            out.append(pv if pvs[mp] is None else pvs[mp] + pv)
        return tuple(out)

    def stage(i, j, s_cur, bm_cur, m, i_next, j_next, s_next):
        c = block_offset(i, j)
        mn, alpha, m_new = [], [], []
        for mp in range(2):
            msh = m[mp] - c
            x = jnp.maximum(msh, bm_cur[mp])
            mn.append(x)
            alpha.append(jnp.exp2(msh - x))
            m_new.append(x + c)
        bm_next = (neg, neg)
        pvs = (None, None)
        nxt = score_units(i_next, j_next, s_next)
        for u in range(n_units):
            bm_next = next(nxt)
            pvs = value_unit(j, u, s_cur, mn, pvs)
        for mp in range(2):
            accs[mp][...] = alpha[mp] * accs[mp][...] + pvs[mp]
        return bm_next, tuple(m_new)

    def run_stages(i, j0, count, bm, m, last_next):
        for d in range(count):
            j = j0 + d
            cur, oth = (s_even, s_odd) if d % 2 == 0 else (s_odd, s_even)
            i_n, j_n = (i, j + 1) if d + 1 < count or last_next is None else last_next
            bm, m = stage(i, j, cur, bm, m, i_n, j_n, oth)
        return bm, m

    def finalize(i):
        a1 = acc1[...]
        a2 = acc2[...]
        a = a1[:C_VDIM] / a1[C_VDIM:C_VDIM + 1] - lam * (a2[:C_VDIM] / a2[C_VDIM:C_VDIM + 1])
        ms = jnp.mean(a * a, axis=0, keepdims=True)
        y = a * lax.rsqrt(ms + RMS_EPS) * g_ref[...] * (1.0 - lam_init)
        o_ref[pl.ds(pl.multiple_of(i * C_TQ, C_TQ), C_TQ), :] = y.T.astype(o_ref.dtype)

    def q_block(i, bm):
        m = (neg, neg)
        trips = n_kv // C_STAGES - 1
        if trips > 0:
            def trip(t, carry):
                return run_stages(i, t * C_STAGES, C_STAGES, carry[0], carry[1], None)
            bm, m = lax.fori_loop(0, trips, trip, (bm, m))
        i_next = jnp.minimum(i + 1, n_q - 1)
        bm, _ = run_stages(i, n_kv - C_STAGES, C_STAGES, bm, m, (i_next, 0))
        finalize(i)
        return bm

    bm0 = (neg, neg)
    for bm0 in score_units(0, 0, s_even):
        pass
    lax.fori_loop(0, n_q, q_block, bm0)


def _c_call(scal, lamv, subln_g, qct, kc, vct, batch, seq):
    t = batch * seq
    n_q = seq // C_TQ
    n_kv = seq // C_TK
    kw = 2 * HEAD_DIM
    assert n_kv % C_STAGES == 0 and C_STAGES % 2 == 0
    bias = jnp.asarray(_c_bias())
    g = jnp.broadcast_to(subln_g.astype(F32)[:, None], (C_VDIM, C_TQ))
    return pl.pallas_call(
        functools.partial(_c_kernel, n_q=n_q, n_kv=n_kv),
        grid=(batch, C_HEADS),
        in_specs=[pl.BlockSpec(memory_space=pltpu.SMEM),
                  _const_spec((4, HEAD_DIM)),
                  pl.BlockSpec((n_q, kw, C_TQ), lambda b, h: (b, h, 0)),
                  pl.BlockSpec((seq, kw), lambda b, h: (b, h)),
                  pl.BlockSpec((n_kv, C_VDIM, C_TK), lambda b, h: (b, h, 0)),
                  pl.BlockSpec((1,) + bias.shape[1:], lambda b, h: (h, 0, 0, 0)),
                  _const_spec((C_VDIM, C_TQ))],
        out_specs=pl.BlockSpec((seq, C_VDIM), lambda b, h: (b, h)),
        out_shape=jax.ShapeDtypeStruct((t, C_WIDTH), BF16),
        scratch_shapes=[pltpu.VMEM((2, C_TK, C_TQ), F32), pltpu.VMEM((2, C_TK, C_TQ), F32),
                        pltpu.VMEM((C_VDIM + C_ONES_ROWS, C_TQ), F32),
                        pltpu.VMEM((C_VDIM + C_ONES_ROWS, C_TQ), F32)],
        compiler_params=pltpu.CompilerParams(
            dimension_semantics=("parallel", "parallel"), vmem_limit_bytes=VMEM_LIMIT),
        name="mixer_c",
    )(scal, lamv, qct, kc, vct, bias, g)


def _sigmoid(x):
    return 1.0 / (1.0 + jnp.exp(-x))


def _post_kernel(x_ref, g_ref, oa0_ref, oa1_ref, oa2_ref, la0_ref, la1_ref, la2_ref, ob_ref, oc_ref,
                 wg_ref, woa_ref, wob_ref, woc_ref, wout_ref, fg_ref, y_ref, tok_scr, *, final):
    x = x_ref[...]
    h = _rms(x, g_ref[...]).astype(BF16)

    def token_order(ref, slot):
        _, dil, n, width = ref.shape
        if dil == 1:
            return ref[0, 0]
        n_chunks = width // LANES
        for r in range(dil):
            rows = ref[0, r].astype(F32)
            for c in range(n_chunks):
                tok_scr[slot, c, pl.ds(r, n, stride=dil), :] = rows[:, c * LANES:(c + 1) * LANES]
        return jnp.concatenate([tok_scr[slot, c] for c in range(n_chunks)], axis=1)

    def gate(c0, n):
        return jnp.dot(h, wg_ref[:, c0:c0 + n], preferred_element_type=F32)

    def silu_gated(o, c0):
        ga = gate(c0, o.shape[1])
        return (o * (ga * _sigmoid(ga))).astype(BF16)

    l0, l1, l2 = token_order(la0_ref, 0), token_order(la1_ref, 0), token_order(la2_ref, 1)
    o0, o1, o2 = token_order(oa0_ref, 0), token_order(oa1_ref, 2), token_order(oa2_ref, 3)
    m = jnp.maximum(jnp.maximum(l0, l1), l2)
    e0, e1, e2 = jnp.exp2(l0 - m), jnp.exp2(l1 - m), jnp.exp2(l2 - m)
    oa = (e0 * o0 + e1 * o1 + e2 * o2) / (e0 + e1 + e2)

    ya = jnp.dot(silu_gated(oa, 0), woa_ref[...], preferred_element_type=F32)
    yb = jnp.dot(silu_gated(ob_ref[...], A_WIDTH), wob_ref[...], preferred_element_type=F32)
    yc = jnp.dot(silu_gated(oc_ref[...], A_WIDTH + B_WIDTH), woc_ref[...], preferred_element_type=F32)
    g0 = A_WIDTH + B_WIDTH + C_WIDTH
    mixed = (_sigmoid(gate(g0, D_MODEL)) * ya + _sigmoid(gate(g0 + D_MODEL, D_MODEL)) * yb
             + _sigmoid(gate(g0 + 2 * D_MODEL, D_MODEL)) * yc)
    y = x + jnp.dot(mixed.astype(BF16), wout_ref[...], preferred_element_type=F32)
    if final:
        y = _rms(y, fg_ref[...])
    y_ref[...] = y


def _post_call(x, g, oa, la, ob, oc, w, final_g, final, seq):
    t = x.shape[0]
    tm = POST_TM
    nt = seq // tm
    tok = lambda n: pl.BlockSpec((tm, n), lambda i: (i, 0))
    res = [pl.BlockSpec((1, dil, tm // dil, A_WIDTH), lambda i: (i // nt, 0, i % nt, 0))
           for _, dil in A_PATTERNS]
    return pl.pallas_call(
        functools.partial(_post_kernel, final=final),
        grid=(t // tm,),
        in_specs=[tok(D_MODEL), _const_spec((1, D_MODEL))] + res + res + [tok(B_WIDTH), tok(C_WIDTH)]
                 + [_const_spec(w["g"].shape), _const_spec(w["oa"].shape), _const_spec(w["ob"].shape),
                    _const_spec(w["oc"].shape), _const_spec(w["out"].shape), _const_spec((1, D_MODEL))],
        out_specs=tok(D_MODEL),
        out_shape=jax.ShapeDtypeStruct((t, D_MODEL), F32),
        scratch_shapes=[pltpu.VMEM((4, A_WIDTH // LANES, tm, LANES), F32)],
        compiler_params=pltpu.CompilerParams(dimension_semantics=("parallel",),
                                             vmem_limit_bytes=VMEM_LIMIT),
        name="post_final" if final else "post",
    )(x, g, oa[0], oa[1], oa[2], la[0], la[1], la[2], ob, oc,
      w["g"], w["oa"], w["ob"], w["oc"], w["out"], final_g)


def _layer_weights(w_in, w_oa, w_ob, w_oc, w_out):
    col = lambda idx: w_in[:, IN_OFFSETS[idx]:IN_OFFSETS[idx + 1]]
    scale = HEAD_DIM ** -0.5
    ft = jnp.concatenate([col(4) * (scale * LOG2E), col(6), col(8) * (scale * LOG2E), col(10)], axis=1).T
    grp = lambda idx, gi: col(idx)[:, gi * A_WIDTH:(gi + 1) * A_WIDTH]
    a_qkv = lambda gi: jnp.concatenate([grp(0, gi) * (scale * LOG2E), grp(1, gi), grp(2, gi)],
                                       axis=1).astype(BF16)
    return {
        "a0": a_qkv(0), "a1": a_qkv(1), "a2": a_qkv(2),
        "kb": col(5).astype(BF16), "kc": col(9).astype(BF16), "ft": ft.astype(BF16),
        "g": jnp.concatenate([col(3), col(7), col(11), col(12)], axis=1).astype(BF16),
        "oa": w_oa.astype(BF16), "ob": w_ob.astype(BF16), "oc": w_oc.astype(BF16),
        "out": w_out.astype(BF16),
    }


def _trunk(x3, layers, final_g):
    batch, seq, _ = x3.shape
    x = x3.reshape(batch * seq, D_MODEL)
    fg = final_g.astype(F32).reshape(1, D_MODEL)
    for li, lw in enumerate(layers):
        w = lw["w"]
        a0, a1, a2, kb, kc, qbt, vbt, qct, vct = _proj_call(x, lw["norm_g"], w, batch, seq)
        oa, la = zip(*[_a_call(qkv, gi) for gi, qkv in enumerate((a0, a1, a2))])
        ob = _b_call(lw["sink"], qbt, kb, vbt, batch, seq)
        oc = _c_call(lw["scal"], lw["lamv"], lw["subln_g"], qct, kc, vct, batch, seq)
        x = _post_call(x, lw["norm_g"], oa, la, ob, oc, w, fg, final=(li == len(layers) - 1), seq=seq)
    return x.reshape(batch, seq, D_MODEL)


def _prepare_layers(norm_g, w_in, w_oa, w_ob, w_oc, w_out, b_sink, lam_q1, lam_k1, lam_q2, lam_k2, c_subln_g):
    c_slopes = jnp.asarray(_alibi_slopes(C_HEADS) * np.float32(LOG2E))
    layers = []
    for l in range(DEPTH):
        lam_init = 0.8 - 0.6 * math.exp(-0.3 * l)
        layers.append({
            "w": _layer_weights(w_in[l], w_oa[l], w_ob[l], w_oc[l], w_out[l]),
            "norm_g": norm_g[l].astype(F32).reshape(1, D_MODEL),
            "sink": b_sink[l].astype(F32) * LOG2E,
            "scal": jnp.concatenate([c_slopes, jnp.full((1,), lam_init, F32)]),
            "lamv": jnp.stack([lam_q1[l], lam_k1[l], lam_q2[l], lam_k2[l]]).astype(F32),
            "subln_g": c_subln_g[l],
        })
    return layers


def kernel(x_prompt, x_sample, norm_g, w_in, w_oa, w_ob, w_oc, w_out, b_sink, lam_q1, lam_k1, lam_q2, lam_k2, c_subln_g, final_norm_g):
    layers = _prepare_layers(norm_g, w_in, w_oa, w_ob, w_oc, w_out, b_sink,
                             lam_q1, lam_k1, lam_q2, lam_k2, c_subln_g)
    return (_trunk(x_prompt, layers, final_norm_g), _trunk(x_sample, layers, final_norm_g))
```

```python
import functools
import math

import numpy as np
import jax
import jax.numpy as jnp
from jax import lax
from jax.experimental import pallas as pl
from jax.experimental.pallas import tpu as pltpu

F32 = jnp.float32
BF16 = jnp.bfloat16

D_MODEL = 1024
DEPTH = 4
HEAD_DIM = 64
A_PATTERNS = ((128, 1), (512, 4), (2048, 16))
A_GROUPS = 3
A_HEADS = 8
A_WIDTH = A_HEADS * HEAD_DIM
A_HALF = 64
B_HEADS = 8
B_KV_HEADS = 2
B_HALF = 128
B_WIDTH = B_HEADS * HEAD_DIM
C_HEADS = 4
C_VDIM = 2 * HEAD_DIM
C_WIDTH = C_HEADS * C_VDIM
RMS_EPS = 1e-6
NEG_INF = -1e30
IN_SIZES = (
    A_GROUPS * A_WIDTH, A_GROUPS * A_WIDTH, A_GROUPS * A_WIDTH, A_WIDTH,
    B_WIDTH, B_KV_HEADS * HEAD_DIM, B_KV_HEADS * HEAD_DIM, B_WIDTH,
    2 * C_HEADS * HEAD_DIM, 2 * C_HEADS * HEAD_DIM, C_WIDTH, C_WIDTH,
    3 * D_MODEL,
)
IN_OFFSETS = tuple(int(c) for c in np.cumsum((0,) + IN_SIZES))

PROJ_TM = 512
POST_TM = 256
A_TQ = 128
A_STEP = 1024
B_TQ = 256
B_UNIT = 256
C_TQ = 256
C_TK = 512
C_UNIT = 256
C_STAGES = 8
C_ONES_ROWS = 16
LOG2E = 1.4426950408889634
LANES = 128
VMEM_LIMIT = 56 * 2**20

N_FEAT = B_WIDTH + B_KV_HEADS * HEAD_DIM + 2 * C_HEADS * HEAD_DIM + C_WIDTH


def _alibi_slopes(n):
    return np.asarray([2.0 ** (-8.0 * (i + 1) / n) for i in range(n)], dtype=np.float32)


def _rms(x, g):
    ms = jnp.mean(x * x, axis=-1, keepdims=True)
    return x * lax.rsqrt(ms + RMS_EPS) * g


def _const_spec(shape):
    nd = len(shape)
    return pl.BlockSpec(shape, lambda *_: (0,) * nd, pipeline_mode=pl.Buffered(1))


def _proj_kernel(x_ref, g_ref, wa0_ref, wa1_ref, wa2_ref, wkb_ref, wkc_ref, wft_ref,
                 a0_ref, a1_ref, a2_ref, kb_ref, kc_ref, qbt_ref, vbt_ref, qct_ref, vct_ref, h_scr):
    tm = x_ref.shape[0]
    h32 = _rms(x_ref[...], g_ref[...])
    h = h32.astype(BF16)
    a0_ref[0, 0] = jnp.dot(h, wa0_ref[...], preferred_element_type=F32).astype(BF16)
    n_chunks = h_scr.shape[0]
    for c in range(n_chunks):
        h_scr[c] = h32[:, c * LANES:(c + 1) * LANES]
    for w_ref, o_ref, (_, dil) in ((wa1_ref, a1_ref, A_PATTERNS[1]), (wa2_ref, a2_ref, A_PATTERNS[2])):
        n = tm // dil
        hp = jnp.concatenate(
            [jnp.concatenate([h_scr[c, pl.ds(r, n, stride=dil), :] for c in range(n_chunks)], axis=1)
             for r in range(dil)], axis=0).astype(BF16)
        res = jnp.dot(hp, w_ref[...], preferred_element_type=F32).astype(BF16)
        for r in range(dil):
            o_ref[0, r] = res[r * n:(r + 1) * n]
    for w_ref, o_ref in ((wkb_ref, kb_ref), (wkc_ref, kc_ref)):
        o_ref[...] = jnp.dot(h, w_ref[...], preferred_element_type=F32).astype(BF16)
    ft = lax.dot_general(wft_ref[...], h, (((1,), (1,)), ((), ())),
                         preferred_element_type=F32).astype(BF16)
    r0 = 0
    for o_ref, rows in ((qbt_ref, B_WIDTH), (vbt_ref, B_KV_HEADS * HEAD_DIM),
                        (qct_ref, 2 * C_HEADS * HEAD_DIM), (vct_ref, C_WIDTH)):
        nblk, _, width = o_ref.shape
        for c in range(nblk):
            o_ref[c] = ft[r0:r0 + rows, c * width:(c + 1) * width]
        r0 += rows


def _proj_call(x, g, w, batch, seq):
    t = x.shape[0]
    tm = PROJ_TM
    nt = seq // tm
    tok = lambda n: pl.BlockSpec((tm, n), lambda i: (i, 0))
    feat = lambda rows, width: pl.BlockSpec((tm // width, rows, width), lambda i: (i, 0, 0))
    a_cols = 3 * A_WIDTH
    res = lambda dil: pl.BlockSpec((1, dil, tm // dil, a_cols), lambda i: (i // nt, 0, i % nt, 0))
    a_shape = lambda dil: jax.ShapeDtypeStruct((batch, dil, seq // dil, a_cols), BF16)
    dils = [d for _, d in A_PATTERNS]
    out_shape = (
        a_shape(dils[0]), a_shape(dils[1]), a_shape(dils[2]),
        jax.ShapeDtypeStruct((t, B_KV_HEADS * HEAD_DIM), BF16),
        jax.ShapeDtypeStruct((t, 2 * C_HEADS * HEAD_DIM), BF16),
        jax.ShapeDtypeStruct((t // B_TQ, B_WIDTH, B_TQ), BF16),
        jax.ShapeDtypeStruct((t // B_HALF, B_KV_HEADS * HEAD_DIM, B_HALF), BF16),
        jax.ShapeDtypeStruct((t // C_TQ, 2 * C_HEADS * HEAD_DIM, C_TQ), BF16),
        jax.ShapeDtypeStruct((t // C_TK, C_WIDTH, C_TK), BF16),
    )
    return pl.pallas_call(
        _proj_kernel,
        grid=(t // tm,),
        in_specs=[tok(D_MODEL), _const_spec((1, D_MODEL)),
                  _const_spec(w["a0"].shape), _const_spec(w["a1"].shape), _const_spec(w["a2"].shape),
                  _const_spec(w["kb"].shape), _const_spec(w["kc"].shape), _const_spec(w["ft"].shape)],
        out_specs=(res(dils[0]), res(dils[1]), res(dils[2]), tok(B_KV_HEADS * HEAD_DIM),
                   tok(2 * C_HEADS * HEAD_DIM),
                   feat(B_WIDTH, B_TQ), feat(B_KV_HEADS * HEAD_DIM, B_HALF),
                   feat(2 * C_HEADS * HEAD_DIM, C_TQ), feat(C_WIDTH, C_TK)),
        out_shape=out_shape,
        scratch_shapes=[pltpu.VMEM((D_MODEL // LANES, tm, LANES), F32)],
        compiler_params=pltpu.CompilerParams(dimension_semantics=("parallel",),
                                             vmem_limit_bytes=VMEM_LIMIT),
        name="proj",
    )(x, g, w["a0"], w["a1"], w["a2"], w["kb"], w["kc"], w["ft"])


def _a_bias(gi):
    dil = A_PATTERNS[gi][1]
    slopes = _alibi_slopes(A_GROUPS * A_HEADS).reshape(A_GROUPS, A_HEADS)[gi].astype(np.float64) * LOG2E
    cols = np.arange(A_TQ + 2 * A_HALF)[None, :]
    rel = (cols - A_HALF) - np.arange(A_TQ)[:, None]
    bias = -slopes[:, None, None] * (np.abs(rel) * float(dil))[None]
    out = []
    for variant in range(4):
        ok = np.abs(rel) <= A_HALF
        if variant & 1:
            ok = ok & (cols >= A_HALF)
        if variant & 2:
            ok = ok & (cols < A_TQ + A_HALF)
        out.append(np.where(ok[None], bias, NEG_INF))
    return np.stack(out).astype(np.float32)


def _a_kernel(cur_ref, prev_ref, next_ref, bias_first_ref, bias_mid_ref, bias_last_ref,
              o_ref, lse_ref, kfull, vfull, *, step, n_res):
    tkw = A_TQ + 2 * A_HALF
    for rr in range(n_res):
        for full, k0 in ((kfull, A_WIDTH), (vfull, 2 * A_WIDTH)):
            full[rr, 0:A_HALF] = prev_ref[0, rr, :, k0:k0 + A_WIDTH]
            full[rr, A_HALF:A_HALF + step] = cur_ref[0, rr, :, k0:k0 + A_WIDTH]
            full[rr, A_HALF + step:] = next_ref[0, rr, :, k0:k0 + A_WIDTH]
    n_sb = step // A_TQ
    lane = lax.broadcasted_iota(jnp.int32, (tkw, 2 * HEAD_DIM), 1)
    lane_o = lax.broadcasted_iota(jnp.int32, (A_TQ, 2 * HEAD_DIM), 1)
    tasks = [(rr, sb, p, hh) for rr in range(n_res) for sb in range(n_sb)
             for p in range(A_HEADS // 2) for hh in range(2)]

    def in_half(hh):
        return (lane >= hh * HEAD_DIM) & (lane < (hh + 1) * HEAD_DIM)

    def scores(task):
        rr, sb, p, hh = task
        r0, c0 = sb * A_TQ, p * 2 * HEAD_DIM
        bias_ref = bias_first_ref if sb == 0 else (bias_last_ref if sb == n_sb - 1 else bias_mid_ref)
        qp = cur_ref[0, rr, r0:r0 + A_TQ, c0:c0 + 2 * HEAD_DIM]
        kw = kfull[rr, r0:r0 + tkw, c0:c0 + 2 * HEAD_DIM]
        kz = jnp.where(in_half(hh), kw, jnp.zeros_like(kw))
        s = lax.dot_general(qp, kz, (((1,), (1,)), ((), ())), preferred_element_type=F32)
        s = s + bias_ref[0, 2 * p + hh]
        return s, jnp.max(s, axis=1, keepdims=True)

    def values(task, s, m):
        rr, sb, p, hh = task
        r0, c0 = sb * A_TQ, p * 2 * HEAD_DIM
        vw = vfull[rr, r0:r0 + tkw, c0:c0 + 2 * HEAD_DIM]
        vz = jnp.where(in_half(hh), vw, jnp.zeros_like(vw))
        e = jnp.exp2(s - m)
        l = jnp.sum(e, axis=1, keepdims=True)
        o = jnp.dot(e.astype(BF16), vz, preferred_element_type=F32) / l
        return o, jnp.broadcast_to(m + jnp.log(l) * LOG2E, (A_TQ, 2 * HEAD_DIM))

    nxt = scores(tasks[0])
    o_pair = lse_pair = None
    for ti, task in enumerate(tasks):
        s, m = nxt
        if ti + 1 < len(tasks):
            nxt = scores(tasks[ti + 1])
        o, lse = values(task, s, m)
        rr, sb, p, hh = task
        if hh == 0:
            o_pair, lse_pair = o, lse
        else:
            r0, c0 = sb * A_TQ, p * 2 * HEAD_DIM
            o_ref[0, rr, r0:r0 + A_TQ, c0:c0 + 2 * HEAD_DIM] = (o_pair + o).astype(o_ref.dtype)
            lse_ref[0, rr, r0:r0 + A_TQ, c0:c0 + 2 * HEAD_DIM] = jnp.where(lane_o < HEAD_DIM, lse_pair, lse)


def _a_call(qkv, gi):
    batch, dil, sub, cols = qkv.shape
    step = min(A_STEP, sub)
    n_res = min(dil, A_STEP // step)
    n_steps = sub // step
    hb = step // A_HALF
    nhb = sub // A_HALF
    cur = lambda width: pl.BlockSpec((1, n_res, step, width), lambda b, r, u: (b, r, u, 0))
    prev = pl.BlockSpec((1, n_res, A_HALF, cols), lambda b, r, u: (b, r, jnp.maximum(u * hb - 1, 0), 0))
    nxt = pl.BlockSpec((1, n_res, A_HALF, cols),
                       lambda b, r, u: (b, r, jnp.minimum((u + 1) * hb, nhb - 1), 0))
    bias = jnp.asarray(_a_bias(gi))
    at_start = lambda u: (u == 0).astype(jnp.int32)
    at_end = lambda u: 2 * (u == n_steps - 1).astype(jnp.int32)
    single = step == A_TQ
    bias_spec = lambda variant: pl.BlockSpec((1,) + bias.shape[1:], lambda b, r, u: (variant(u), 0, 0, 0))
    return pl.pallas_call(
        functools.partial(_a_kernel, step=step, n_res=n_res),
        grid=(batch, dil // n_res, n_steps),
        in_specs=[cur(cols), prev, nxt,
                  bias_spec(lambda u: at_start(u) + (at_end(u) if single else 0)),
                  bias_spec(lambda u: 0 * u), bias_spec(at_end)],
        out_specs=(cur(A_WIDTH), cur(A_WIDTH)),
        out_shape=(jax.ShapeDtypeStruct((batch, dil, sub, A_WIDTH), BF16),
                   jax.ShapeDtypeStruct((batch, dil, sub, A_WIDTH), F32)),
        scratch_shapes=[pltpu.VMEM((n_res, step + 2 * A_HALF, A_WIDTH), BF16),
                        pltpu.VMEM((n_res, step + 2 * A_HALF, A_WIDTH), BF16)],
        compiler_params=pltpu.CompilerParams(
            dimension_semantics=("parallel", "parallel", "parallel"), vmem_limit_bytes=VMEM_LIMIT),
        name=f"mixer_a{gi}",
    )(qkv, qkv, qkv, bias, bias, bias)


def _b_bias():
    slopes = _alibi_slopes(B_HEADS).astype(np.float64) * LOG2E
    rows = np.arange(B_TQ + 2 * B_HALF)[:, None]
    rel = (rows - B_HALF) - np.arange(B_TQ)[None, :]
    bias = -slopes[:, None, None] * np.abs(rel)[None]
    out = []
    for variant in range(4):
        ok = np.abs(rel) <= B_HALF
        if variant & 1:
            ok = ok & (rows >= B_HALF)
        if variant & 2:
            ok = ok & (rows < B_HALF + B_TQ)
        out.append(np.where(ok[None], bias, NEG_INF))
    return np.stack(out).astype(np.float32)


def _b_kernel(sink_ref, q_ref, kp_ref, kc_ref, kn_ref, vp_ref, vc_ref, vn_ref, bias_ref,
              o_ref, kfull, vfull, s_even, s_odd):
    tk = B_TQ + 2 * B_HALF
    n_units = tk // B_UNIT
    kfull[0:B_HALF] = kp_ref[...]
    kfull[B_HALF:B_HALF + B_TQ] = kc_ref[...]
    kfull[B_HALF + B_TQ:] = kn_ref[...]
    vfull[:, 0:B_HALF] = vp_ref[0]
    for c in range(B_TQ // B_HALF):
        vfull[:, (c + 1) * B_HALF:(c + 2) * B_HALF] = vc_ref[c]
    vfull[:, B_HALF + B_TQ:] = vn_ref[0]
    zeros = jnp.zeros((HEAD_DIM, B_TQ), BF16)
    ones_rows = (lax.broadcasted_iota(jnp.int32, (C_ONES_ROWS, B_UNIT), 0) == 0).astype(BF16)
    neg = jnp.full((1, B_TQ), NEG_INF, F32)
    grp = B_HEADS // B_KV_HEADS

    def score_unit(h, u, s_scr, bm):
        r0 = u * B_UNIT
        qh = q_ref[0, h * HEAD_DIM:(h + 1) * HEAD_DIM, :]
        rhs = jnp.concatenate([qh, zeros] if h // grp == 0 else [zeros, qh], axis=0)
        s = (jnp.dot(kfull[r0:r0 + B_UNIT, :], rhs, preferred_element_type=F32)
             + bias_ref[0, h, r0:r0 + B_UNIT, :])
        s_scr[r0:r0 + B_UNIT, :] = s
        return jnp.maximum(bm, jnp.max(s, axis=0, keepdims=True))

    def value_unit(h, u, s_scr, m, pv):
        r0 = u * B_UNIT
        kvh = h // grp
        v = jnp.concatenate([vfull[kvh * HEAD_DIM:(kvh + 1) * HEAD_DIM, r0:r0 + B_UNIT], ones_rows],
                            axis=0)
        e = jnp.exp2(s_scr[r0:r0 + B_UNIT, :] - m)
        out = jnp.dot(v, e.astype(BF16), preferred_element_type=F32)
        return out if pv is None else pv + out

    scr = (s_even, s_odd)
    bm = neg
    for u in range(n_units):
        bm = score_unit(0, u, s_even, bm)
    outs = []
    for h in range(B_HEADS):
        sink = sink_ref[h]
        m = jnp.maximum(bm, sink)
        bm = neg
        pv = None
        for u in range(n_units):
            if h + 1 < B_HEADS:
                bm = score_unit(h + 1, u, scr[(h + 1) % 2], bm)
            pv = value_unit(h, u, scr[h % 2], m, pv)
        l = pv[HEAD_DIM:HEAD_DIM + 1] + jnp.exp2(sink - m)
        outs.append(pv[:HEAD_DIM] / l)
    o_ref[...] = jnp.concatenate(outs, axis=0).T.astype(o_ref.dtype)


def _b_call(sink2, qbt, kb, vbt, batch, seq):
    t = batch * seq
    n_q = seq // B_TQ
    nkb = seq // B_HALF
    r = B_TQ // B_HALF
    kvw = B_KV_HEADS * HEAD_DIM
    tk = B_TQ + 2 * B_HALF
    prev_i = lambda b, i: b * nkb + jnp.maximum(i * r - 1, 0)
    next_i = lambda b, i: b * nkb + jnp.minimum((i + 1) * r, nkb - 1)
    variant = lambda b, i: ((i == 0).astype(jnp.int32) + 2 * (i == n_q - 1).astype(jnp.int32), 0, 0, 0)
    bias = jnp.asarray(_b_bias())
    return pl.pallas_call(
        _b_kernel,
        grid=(batch, n_q),
        in_specs=[pl.BlockSpec(memory_space=pltpu.SMEM),
                  pl.BlockSpec((1, B_WIDTH, B_TQ), lambda b, i: (b * n_q + i, 0, 0)),
                  pl.BlockSpec((B_HALF, kvw), lambda b, i: (prev_i(b, i), 0)),
                  pl.BlockSpec((B_TQ, kvw), lambda b, i: (b * n_q + i, 0)),
                  pl.BlockSpec((B_HALF, kvw), lambda b, i: (next_i(b, i), 0)),
                  pl.BlockSpec((1, kvw, B_HALF), lambda b, i: (prev_i(b, i), 0, 0)),
                  pl.BlockSpec((r, kvw, B_HALF), lambda b, i: (b * n_q + i, 0, 0)),
                  pl.BlockSpec((1, kvw, B_HALF), lambda b, i: (next_i(b, i), 0, 0)),
                  pl.BlockSpec((1,) + bias.shape[1:], variant)],
        out_specs=pl.BlockSpec((B_TQ, B_WIDTH), lambda b, i: (b * n_q + i, 0)),
        out_shape=jax.ShapeDtypeStruct((t, B_WIDTH), BF16),
        scratch_shapes=[pltpu.VMEM((tk, kvw), BF16), pltpu.VMEM((kvw, tk), BF16),
                        pltpu.VMEM((tk, B_TQ), F32), pltpu.VMEM((tk, B_TQ), F32)],
        compiler_params=pltpu.CompilerParams(dimension_semantics=("parallel", "parallel"),
                                             vmem_limit_bytes=VMEM_LIMIT),
        name="mixer_b",
    )(sink2, qbt, kb, kb, kb, vbt, vbt, vbt, bias)


def _c_bias():
    slopes = _alibi_slopes(C_HEADS).astype(np.float64) * LOG2E
    d = (np.arange(C_TK)[:, None] - np.arange(C_TQ)[None, :]).astype(np.float64)
    tiles = [d, -d]
    for part in range(C_TK // C_TQ):
        tiles.append(-np.abs(d - part * C_TQ))
    return (slopes[:, None, None, None] * np.stack(tiles)[None]).astype(np.float32)


def _c_kernel(scal_ref, lamv_ref, q_ref, k_ref, v_ref, bias_ref, g_ref, o_ref,
              s_even, s_odd, acc1, acc2, *, n_q, n_kv, stages):
    h = pl.program_id(1)
    ratio = C_TK // C_TQ
    n_units = C_TK // C_UNIT
    slope2 = scal_ref[h]
    lam_init = scal_ref[C_HEADS]
    accs = (acc1, acc2)
    zeros = jnp.zeros((HEAD_DIM, C_TQ), BF16)
    ones_rows = (lax.broadcasted_iota(jnp.int32, (C_ONES_ROWS, C_UNIT), 0) == 0).astype(BF16)
    neg = jnp.full((1, C_TQ), NEG_INF, F32)
    lv = lamv_ref[...]
    lam = (jnp.exp(jnp.sum(lv[0:1] * lv[1:2], axis=1, keepdims=True))
           - jnp.exp(jnp.sum(lv[2:3] * lv[3:4], axis=1, keepdims=True)) + lam_init)
    acc1[...] = jnp.zeros_like(acc1)
    acc2[...] = jnp.zeros_like(acc2)

    def block_offset(i, j):
        return jnp.where(j == i // ratio, 0.0, -slope2 * jnp.abs(C_TQ * i - C_TK * j).astype(F32))

    def score_units(i, j, s_scr):
        jd = i // ratio
        tile = jnp.where(j < jd, 0, jnp.where(j > jd, 1, 2 + i % ratio))
        q = q_ref[i]
        rhs = (jnp.concatenate([q[:HEAD_DIM], zeros], axis=0),
               jnp.concatenate([zeros, q[HEAD_DIM:]], axis=0))
        bm = [neg, neg]
        for u in range(n_units):
            r0 = u * C_UNIT
            k = k_ref[pl.ds(pl.multiple_of(j * C_TK + r0, C_UNIT), C_UNIT), :]
            bias = bias_ref[0, tile, r0:r0 + C_UNIT, :]
            for mp in range(2):
                s = jnp.dot(k, rhs[mp], preferred_element_type=F32) + bias
                s_scr[mp, r0:r0 + C_UNIT, :] = s
                bm[mp] = jnp.maximum(bm[mp], jnp.max(s, axis=0, keepdims=True))
            yield tuple(bm)

    def value_unit(j, u, s_scr, mn, pvs):
        r0 = u * C_UNIT
        v = jnp.concatenate([v_ref[j, :, r0:r0 + C_UNIT], ones_rows], axis=0)
        out = []
        for mp in range(2):
            e = jnp.exp2(s_scr[mp, r0:r0 + C_UNIT, :] - mn[mp])
            pv = jnp.dot(v, e.astype(BF16), preferred_element_type=F32)
            out.append(pv if pvs[mp] is None else pvs[mp] + pv)
        return tuple(out)

    def stage(i, j, s_cur, bm_cur, m, i_next, j_next, s_next):
        c = block_offset(i, j)
        mn, alpha, m_new = [], [], []
        for mp in range(2):
            msh = m[mp] - c
            x = jnp.maximum(msh, bm_cur[mp])
            mn.append(x)
            alpha.append(jnp.exp2(msh - x))
            m_new.append(x + c)
        bm_next = (neg, neg)
        pvs = (None, None)
        nxt = score_units(i_next, j_next, s_next)
        for u in range(n_units):
            bm_next = next(nxt)
            pvs = value_unit(j, u, s_cur, mn, pvs)
        for mp in range(2):
            accs[mp][...] = alpha[mp] * accs[mp][...] + pvs[mp]
        return bm_next, tuple(m_new)

    def run_stages(i, j0, count, bm, m, last_next):
        for d in range(count):
            j = j0 + d
            cur, oth = (s_even, s_odd) if d % 2 == 0 else (s_odd, s_even)
            i_n, j_n = (i, j + 1) if d + 1 < count or last_next is None else last_next
            bm, m = stage(i, j, cur, bm, m, i_n, j_n, oth)
        return bm, m

    def finalize(i):
        a1 = acc1[...]
        a2 = acc2[...]
        a = a1[:C_VDIM] / a1[C_VDIM:C_VDIM + 1] - lam * (a2[:C_VDIM] / a2[C_VDIM:C_VDIM + 1])
        ms = jnp.mean(a * a, axis=0, keepdims=True)
        y = a * lax.rsqrt(ms + RMS_EPS) * g_ref[...] * (1.0 - lam_init)
        o_ref[pl.ds(pl.multiple_of(i * C_TQ, C_TQ), C_TQ), :] = y.T.astype(o_ref.dtype)

    def q_block(i, bm):
        m = (neg, neg)
        trips = n_kv // stages - 1
        if trips > 0:
            def trip(t, carry):
                return run_stages(i, t * stages, stages, carry[0], carry[1], None)
            bm, m = lax.fori_loop(0, trips, trip, (bm, m))
        i_next = jnp.minimum(i + 1, n_q - 1)
        bm, _ = run_stages(i, n_kv - stages, stages, bm, m, (i_next, 0))
        finalize(i)
        return bm

    bm0 = (neg, neg)
    for bm0 in score_units(0, 0, s_even):
        pass
    lax.fori_loop(0, n_q, q_block, bm0)


def _c_call(scal, lamv, subln_g, qct, kc, vct, batch, seq):
    t = batch * seq
    n_q = seq // C_TQ
    n_kv = seq // C_TK
    kw = 2 * HEAD_DIM
    stages = min(C_STAGES, n_kv)
    assert n_kv % stages == 0 and stages % 2 == 0
    bias = jnp.asarray(_c_bias())
    g = jnp.broadcast_to(subln_g.astype(F32)[:, None], (C_VDIM, C_TQ))
    return pl.pallas_call(
        functools.partial(_c_kernel, n_q=n_q, n_kv=n_kv, stages=stages),
        grid=(batch, C_HEADS),
        in_specs=[pl.BlockSpec(memory_space=pltpu.SMEM),
                  _const_spec((4, HEAD_DIM)),
                  pl.BlockSpec((n_q, kw, C_TQ), lambda b, h: (b, h, 0)),
                  pl.BlockSpec((seq, kw), lambda b, h: (b, h)),
                  pl.BlockSpec((n_kv, C_VDIM, C_TK), lambda b, h: (b, h, 0)),
                  pl.BlockSpec((1,) + bias.shape[1:], lambda b, h: (h, 0, 0, 0)),
                  _const_spec((C_VDIM, C_TQ))],
        out_specs=pl.BlockSpec((seq, C_VDIM), lambda b, h: (b, h)),
        out_shape=jax.ShapeDtypeStruct((t, C_WIDTH), BF16),
        scratch_shapes=[pltpu.VMEM((2, C_TK, C_TQ), F32), pltpu.VMEM((2, C_TK, C_TQ), F32),
                        pltpu.VMEM((C_VDIM + C_ONES_ROWS, C_TQ), F32),
                        pltpu.VMEM((C_VDIM + C_ONES_ROWS, C_TQ), F32)],
        compiler_params=pltpu.CompilerParams(
            dimension_semantics=("parallel", "parallel"), vmem_limit_bytes=VMEM_LIMIT),
        name="mixer_c",
    )(scal, lamv, qct, kc, vct, bias, g)


def _sigmoid(x):
    return 1.0 / (1.0 + jnp.exp(-x))


def _post_kernel(x_ref, g_ref, oa0_ref, oa1_ref, oa2_ref, la0_ref, la1_ref, la2_ref, ob_ref, oc_ref,
                 wg_ref, woa_ref, wob_ref, woc_ref, wout_ref, fg_ref, y_ref, tok_scr, *, final):
    x = x_ref[...]
    h = _rms(x, g_ref[...]).astype(BF16)

    def token_order(ref, slot):
        _, dil, n, width = ref.shape
        if dil == 1:
            return ref[0, 0]
        n_chunks = width // LANES
        for r in range(dil):
            rows = ref[0, r].astype(F32)
            for c in range(n_chunks):
                tok_scr[slot, c, pl.ds(r, n, stride=dil), :] = rows[:, c * LANES:(c + 1) * LANES]
        return jnp.concatenate([tok_scr[slot, c] for c in range(n_chunks)], axis=1)

    def gate(c0, n):
        return jnp.dot(h, wg_ref[:, c0:c0 + n], preferred_element_type=F32)

    def silu_gated(o, c0):
        ga = gate(c0, o.shape[1])
        return (o * (ga * _sigmoid(ga))).astype(BF16)

    l0, l1, l2 = token_order(la0_ref, 0), token_order(la1_ref, 0), token_order(la2_ref, 1)
    o0, o1, o2 = token_order(oa0_ref, 0), token_order(oa1_ref, 2), token_order(oa2_ref, 3)
    m = jnp.maximum(jnp.maximum(l0, l1), l2)
    e0, e1, e2 = jnp.exp2(l0 - m), jnp.exp2(l1 - m), jnp.exp2(l2 - m)
    oa = (e0 * o0 + e1 * o1 + e2 * o2) / (e0 + e1 + e2)

    ya = jnp.dot(silu_gated(oa, 0), woa_ref[...], preferred_element_type=F32)
    yb = jnp.dot(silu_gated(ob_ref[...], A_WIDTH), wob_ref[...], preferred_element_type=F32)
    yc = jnp.dot(silu_gated(oc_ref[...], A_WIDTH + B_WIDTH), woc_ref[...], preferred_element_type=F32)
    g0 = A_WIDTH + B_WIDTH + C_WIDTH
    mixed = (_sigmoid(gate(g0, D_MODEL)) * ya + _sigmoid(gate(g0 + D_MODEL, D_MODEL)) * yb
             + _sigmoid(gate(g0 + 2 * D_MODEL, D_MODEL)) * yc)
    y = x + jnp.dot(mixed.astype(BF16), wout_ref[...], preferred_element_type=F32)
    if final:
        y = _rms(y, fg_ref[...])
    y_ref[...] = y


def _post_call(x, g, oa, la, ob, oc, w, final_g, final, seq):
    t = x.shape[0]
    tm = POST_TM
    nt = seq // tm
    tok = lambda n: pl.BlockSpec((tm, n), lambda i: (i, 0))
    res = [pl.BlockSpec((1, dil, tm // dil, A_WIDTH), lambda i: (i // nt, 0, i % nt, 0))
           for _, dil in A_PATTERNS]
    return pl.pallas_call(
        functools.partial(_post_kernel, final=final),
        grid=(t // tm,),
        in_specs=[tok(D_MODEL), _const_spec((1, D_MODEL))] + res + res + [tok(B_WIDTH), tok(C_WIDTH)]
                 + [_const_spec(w["g"].shape), _const_spec(w["oa"].shape), _const_spec(w["ob"].shape),
                    _const_spec(w["oc"].shape), _const_spec(w["out"].shape), _const_spec((1, D_MODEL))],
        out_specs=tok(D_MODEL),
        out_shape=jax.ShapeDtypeStruct((t, D_MODEL), F32),
        scratch_shapes=[pltpu.VMEM((4, A_WIDTH // LANES, tm, LANES), F32)],
        compiler_params=pltpu.CompilerParams(dimension_semantics=("parallel",),
                                             vmem_limit_bytes=VMEM_LIMIT),
        name="post_final" if final else "post",
    )(x, g, oa[0], oa[1], oa[2], la[0], la[1], la[2], ob, oc,
      w["g"], w["oa"], w["ob"], w["oc"], w["out"], final_g)


def _layer_weights(w_in, w_oa, w_ob, w_oc, w_out):
    col = lambda idx: w_in[:, IN_OFFSETS[idx]:IN_OFFSETS[idx + 1]]
    scale = HEAD_DIM ** -0.5
    ft = jnp.concatenate([col(4) * (scale * LOG2E), col(6), col(8) * (scale * LOG2E), col(10)], axis=1).T
    grp = lambda idx, gi: col(idx)[:, gi * A_WIDTH:(gi + 1) * A_WIDTH]
    a_qkv = lambda gi: jnp.concatenate([grp(0, gi) * (scale * LOG2E), grp(1, gi), grp(2, gi)],
                                       axis=1).astype(BF16)
    return {
        "a0": a_qkv(0), "a1": a_qkv(1), "a2": a_qkv(2),
        "kb": col(5).astype(BF16), "kc": col(9).astype(BF16), "ft": ft.astype(BF16),
        "g": jnp.concatenate([col(3), col(7), col(11), col(12)], axis=1).astype(BF16),
        "oa": w_oa.astype(BF16), "ob": w_ob.astype(BF16), "oc": w_oc.astype(BF16),
        "out": w_out.astype(BF16),
    }


def _trunk(x3, layers, final_g):
    batch, seq, _ = x3.shape
    x = x3.reshape(batch * seq, D_MODEL)
    fg = final_g.astype(F32).reshape(1, D_MODEL)
    for li, lw in enumerate(layers):
        w = lw["w"]
        a0, a1, a2, kb, kc, qbt, vbt, qct, vct = _proj_call(x, lw["norm_g"], w, batch, seq)
        oa, la = zip(*[_a_call(qkv, gi) for gi, qkv in enumerate((a0, a1, a2))])
        ob = _b_call(lw["sink"], qbt, kb, vbt, batch, seq)
        oc = _c_call(lw["scal"], lw["lamv"], lw["subln_g"], qct, kc, vct, batch, seq)
        x = _post_call(x, lw["norm_g"], oa, la, ob, oc, w, fg, final=(li == len(layers) - 1), seq=seq)
    return x.reshape(batch, seq, D_MODEL)


def _prepare_layers(norm_g, w_in, w_oa, w_ob, w_oc, w_out, b_sink, lam_q1, lam_k1, lam_q2, lam_k2, c_subln_g):
    c_slopes = jnp.asarray(_alibi_slopes(C_HEADS) * np.float32(LOG2E))
    layers = []
    for l in range(DEPTH):
        lam_init = 0.8 - 0.6 * math.exp(-0.3 * l)
        layers.append({
            "w": _layer_weights(w_in[l], w_oa[l], w_ob[l], w_oc[l], w_out[l]),
            "norm_g": norm_g[l].astype(F32).reshape(1, D_MODEL),
            "sink": b_sink[l].astype(F32) * LOG2E,
            "scal": jnp.concatenate([c_slopes, jnp.full((1,), lam_init, F32)]),
            "lamv": jnp.stack([lam_q1[l], lam_k1[l], lam_q2[l], lam_k2[l]]).astype(F32),
            "subln_g": c_subln_g[l],
        })
    return layers


def kernel(x_prompt, x_sample, norm_g, w_in, w_oa, w_ob, w_oc, w_out, b_sink, lam_q1, lam_k1, lam_q2, lam_k2, c_subln_g, final_norm_g):
    layers = _prepare_layers(norm_g, w_in, w_oa, w_ob, w_oc, w_out, b_sink,
                             lam_q1, lam_k1, lam_q2, lam_k2, c_subln_g)
    return (_trunk(x_prompt, layers, final_norm_g), _trunk(x_sample, layers, final_norm_g))
```

```python
import functools
import math

import numpy as np
import jax
import jax.numpy as jnp
from jax import lax
from jax.experimental import pallas as pl
from jax.experimental.pallas import tpu as pltpu

F32 = jnp.float32
BF16 = jnp.bfloat16

D_MODEL = 1024
DEPTH = 4
HEAD_DIM = 64
A_PATTERNS = ((128, 1), (512, 4), (2048, 16))
A_GROUPS = 3
A_HEADS = 8
A_WIDTH = A_HEADS * HEAD_DIM
A_HALF = 64
B_HEADS = 8
B_KV_HEADS = 2
B_HALF = 128
B_WIDTH = B_HEADS * HEAD_DIM
C_HEADS = 4
C_VDIM = 2 * HEAD_DIM
C_WIDTH = C_HEADS * C_VDIM
RMS_EPS = 1e-6
NEG_INF = -1e30
IN_SIZES = (
    A_GROUPS * A_WIDTH, A_GROUPS * A_WIDTH, A_GROUPS * A_WIDTH, A_WIDTH,
    B_WIDTH, B_KV_HEADS * HEAD_DIM, B_KV_HEADS * HEAD_DIM, B_WIDTH,
    2 * C_HEADS * HEAD_DIM, 2 * C_HEADS * HEAD_DIM, C_WIDTH, C_WIDTH,
    3 * D_MODEL,
)
IN_OFFSETS = tuple(int(c) for c in np.cumsum((0,) + IN_SIZES))

PROJ_TM = 512
POST_TM = 512
A_TQ = 128
A_STEP = 1024
B_TQ = 256
B_UNIT = 256
C_TQ = 256
C_TK = 512
C_UNIT = 256
C_STAGES = 8
C_ONES_ROWS = 16
LOG2E = 1.4426950408889634
LANES = 128
VMEM_LIMIT = 56 * 2**20

N_FEAT = B_WIDTH + B_KV_HEADS * HEAD_DIM + 2 * C_HEADS * HEAD_DIM + C_WIDTH


def _alibi_slopes(n):
    return np.asarray([2.0 ** (-8.0 * (i + 1) / n) for i in range(n)], dtype=np.float32)


def _rms(x, g):
    ms = jnp.mean(x * x, axis=-1, keepdims=True)
    return x * lax.rsqrt(ms + RMS_EPS) * g


def _const_spec(shape):
    nd = len(shape)
    return pl.BlockSpec(shape, lambda *_: (0,) * nd, pipeline_mode=pl.Buffered(1))


def _proj_kernel(x_ref, g_ref, wa0_ref, wa1_ref, wa2_ref, wkb_ref, wkc_ref, wft_ref,
                 a0_ref, a1_ref, a2_ref, kb_ref, kc_ref, qbt_ref, vbt_ref, qct_ref, vct_ref, h_scr):
    tm = x_ref.shape[0]
    h32 = _rms(x_ref[...], g_ref[...])
    h = h32.astype(BF16)
    a0_ref[0, 0] = jnp.dot(h, wa0_ref[...], preferred_element_type=F32).astype(BF16)
    n_chunks = h_scr.shape[0]
    for c in range(n_chunks):
        h_scr[c] = h32[:, c * LANES:(c + 1) * LANES]
    for w_ref, o_ref, (_, dil) in ((wa1_ref, a1_ref, A_PATTERNS[1]), (wa2_ref, a2_ref, A_PATTERNS[2])):
        n = tm // dil
        hp = jnp.concatenate(
            [jnp.concatenate([h_scr[c, pl.ds(r, n, stride=dil), :] for c in range(n_chunks)], axis=1)
             for r in range(dil)], axis=0).astype(BF16)
        res = jnp.dot(hp, w_ref[...], preferred_element_type=F32).astype(BF16)
        for r in range(dil):
            o_ref[0, r] = res[r * n:(r + 1) * n]
    for w_ref, o_ref in ((wkb_ref, kb_ref), (wkc_ref, kc_ref)):
        o_ref[...] = jnp.dot(h, w_ref[...], preferred_element_type=F32).astype(BF16)
    ft = lax.dot_general(wft_ref[...], h, (((1,), (1,)), ((), ())),
                         preferred_element_type=F32).astype(BF16)
    r0 = 0
    for o_ref, rows in ((qbt_ref, B_WIDTH), (vbt_ref, B_KV_HEADS * HEAD_DIM),
                        (qct_ref, 2 * C_HEADS * HEAD_DIM), (vct_ref, C_WIDTH)):
        nblk, _, width = o_ref.shape
        for c in range(nblk):
            o_ref[c] = ft[r0:r0 + rows, c * width:(c + 1) * width]
        r0 += rows


def _proj_call(x, g, w, batch, seq):
    t = x.shape[0]
    tm = PROJ_TM
    nt = seq // tm
    tok = lambda n: pl.BlockSpec((tm, n), lambda i: (i, 0))
    feat = lambda rows, width: pl.BlockSpec((tm // width, rows, width), lambda i: (i, 0, 0))
    a_cols = 3 * A_WIDTH
    res = lambda dil: pl.BlockSpec((1, dil, tm // dil, a_cols), lambda i: (i // nt, 0, i % nt, 0))
    a_shape = lambda dil: jax.ShapeDtypeStruct((batch, dil, seq // dil, a_cols), BF16)
    dils = [d for _, d in A_PATTERNS]
    out_shape = (
        a_shape(dils[0]), a_shape(dils[1]), a_shape(dils[2]),
        jax.ShapeDtypeStruct((t, B_KV_HEADS * HEAD_DIM), BF16),
        jax.ShapeDtypeStruct((t, 2 * C_HEADS * HEAD_DIM), BF16),
        jax.ShapeDtypeStruct((t // B_TQ, B_WIDTH, B_TQ), BF16),
        jax.ShapeDtypeStruct((t // B_HALF, B_KV_HEADS * HEAD_DIM, B_HALF), BF16),
        jax.ShapeDtypeStruct((t // C_TQ, 2 * C_HEADS * HEAD_DIM, C_TQ), BF16),
        jax.ShapeDtypeStruct((t // C_TK, C_WIDTH, C_TK), BF16),
    )
    return pl.pallas_call(
        _proj_kernel,
        grid=(t // tm,),
        in_specs=[tok(D_MODEL), _const_spec((1, D_MODEL)),
                  _const_spec(w["a0"].shape), _const_spec(w["a1"].shape), _const_spec(w["a2"].shape),
                  _const_spec(w["kb"].shape), _const_spec(w["kc"].shape), _const_spec(w["ft"].shape)],
        out_specs=(res(dils[0]), res(dils[1]), res(dils[2]), tok(B_KV_HEADS * HEAD_DIM),
                   tok(2 * C_HEADS * HEAD_DIM),
                   feat(B_WIDTH, B_TQ), feat(B_KV_HEADS * HEAD_DIM, B_HALF),
                   feat(2 * C_HEADS * HEAD_DIM, C_TQ), feat(C_WIDTH, C_TK)),
        out_shape=out_shape,
        scratch_shapes=[pltpu.VMEM((D_MODEL // LANES, tm, LANES), F32)],
        compiler_params=pltpu.CompilerParams(dimension_semantics=("parallel",),
                                             vmem_limit_bytes=VMEM_LIMIT),
        name="proj",
    )(x, g, w["a0"], w["a1"], w["a2"], w["kb"], w["kc"], w["ft"])


def _a_bias(gi):
    dil = A_PATTERNS[gi][1]
    slopes = _alibi_slopes(A_GROUPS * A_HEADS).reshape(A_GROUPS, A_HEADS)[gi].astype(np.float64) * LOG2E
    cols = np.arange(A_TQ + 2 * A_HALF)[None, :]
    rel = (cols - A_HALF) - np.arange(A_TQ)[:, None]
    bias = -slopes[:, None, None] * (np.abs(rel) * float(dil))[None]
    out = []
    for variant in range(4):
        ok = np.abs(rel) <= A_HALF
        if variant & 1:
            ok = ok & (cols >= A_HALF)
        if variant & 2:
            ok = ok & (cols < A_TQ + A_HALF)
        out.append(np.where(ok[None], bias, NEG_INF))
    return np.stack(out).astype(np.float32)


def _a_kernel(cur_ref, prev_ref, next_ref, bias_first_ref, bias_mid_ref, bias_last_ref,
              o_ref, lse_ref, kfull, vfull, *, step, n_res):
    tkw = A_TQ + 2 * A_HALF
    for rr in range(n_res):
        for full, k0 in ((kfull, A_WIDTH), (vfull, 2 * A_WIDTH)):
            full[rr, 0:A_HALF] = prev_ref[0, rr, :, k0:k0 + A_WIDTH]
            full[rr, A_HALF:A_HALF + step] = cur_ref[0, rr, :, k0:k0 + A_WIDTH]
            full[rr, A_HALF + step:] = next_ref[0, rr, :, k0:k0 + A_WIDTH]
    n_sb = step // A_TQ
    lane = lax.broadcasted_iota(jnp.int32, (tkw, 2 * HEAD_DIM), 1)
    lane_o = lax.broadcasted_iota(jnp.int32, (A_TQ, 2 * HEAD_DIM), 1)
    tasks = [(rr, sb, p, hh) for rr in range(n_res) for sb in range(n_sb)
             for p in range(A_HEADS // 2) for hh in range(2)]

    def in_half(hh):
        return (lane >= hh * HEAD_DIM) & (lane < (hh + 1) * HEAD_DIM)

    def scores(task):
        rr, sb, p, hh = task
        r0, c0 = sb * A_TQ, p * 2 * HEAD_DIM
        bias_ref = bias_first_ref if sb == 0 else (bias_last_ref if sb == n_sb - 1 else bias_mid_ref)
        qp = cur_ref[0, rr, r0:r0 + A_TQ, c0:c0 + 2 * HEAD_DIM]
        kw = kfull[rr, r0:r0 + tkw, c0:c0 + 2 * HEAD_DIM]
        kz = jnp.where(in_half(hh), kw, jnp.zeros_like(kw))
        s = lax.dot_general(qp, kz, (((1,), (1,)), ((), ())), preferred_element_type=F32)
        s = s + bias_ref[0, 2 * p + hh]
        return s, jnp.max(s, axis=1, keepdims=True)

    def values(task, s, m):
        rr, sb, p, hh = task
        r0, c0 = sb * A_TQ, p * 2 * HEAD_DIM
        vw = vfull[rr, r0:r0 + tkw, c0:c0 + 2 * HEAD_DIM]
        vz = jnp.where(in_half(hh), vw, jnp.zeros_like(vw))
        e = jnp.exp2(s - m)
        l = jnp.sum(e, axis=1, keepdims=True)
        o = jnp.dot(e.astype(BF16), vz, preferred_element_type=F32) / l
        return o, jnp.broadcast_to(m + jnp.log(l) * LOG2E, (A_TQ, 2 * HEAD_DIM))

    nxt = scores(tasks[0])
    o_pair = lse_pair = None
    for ti, task in enumerate(tasks):
        s, m = nxt
        if ti + 1 < len(tasks):
            nxt = scores(tasks[ti + 1])
        o, lse = values(task, s, m)
        rr, sb, p, hh = task
        if hh == 0:
            o_pair, lse_pair = o, lse
        else:
            r0, c0 = sb * A_TQ, p * 2 * HEAD_DIM
            o_ref[0, rr, r0:r0 + A_TQ, c0:c0 + 2 * HEAD_DIM] = (o_pair + o).astype(o_ref.dtype)
            lse_ref[0, rr, r0:r0 + A_TQ, c0:c0 + 2 * HEAD_DIM] = jnp.where(lane_o < HEAD_DIM, lse_pair, lse)


def _a_call(qkv, gi):
    batch, dil, sub, cols = qkv.shape
    step = min(A_STEP, sub)
    n_res = min(dil, A_STEP // step)
    n_steps = sub // step
    hb = step // A_HALF
    nhb = sub // A_HALF
    cur = lambda width: pl.BlockSpec((1, n_res, step, width), lambda b, r, u: (b, r, u, 0))
    prev = pl.BlockSpec((1, n_res, A_HALF, cols), lambda b, r, u: (b, r, jnp.maximum(u * hb - 1, 0), 0))
    nxt = pl.BlockSpec((1, n_res, A_HALF, cols),
                       lambda b, r, u: (b, r, jnp.minimum((u + 1) * hb, nhb - 1), 0))
    bias = jnp.asarray(_a_bias(gi))
    at_start = lambda u: (u == 0).astype(jnp.int32)
    at_end = lambda u: 2 * (u == n_steps - 1).astype(jnp.int32)
    single = step == A_TQ
    bias_spec = lambda variant: pl.BlockSpec((1,) + bias.shape[1:], lambda b, r, u: (variant(u), 0, 0, 0))
    return pl.pallas_call(
        functools.partial(_a_kernel, step=step, n_res=n_res),
        grid=(batch, dil // n_res, n_steps),
        in_specs=[cur(cols), prev, nxt,
                  bias_spec(lambda u: at_start(u) + (at_end(u) if single else 0)),
                  bias_spec(lambda u: 0 * u), bias_spec(at_end)],
        out_specs=(cur(A_WIDTH), cur(A_WIDTH)),
        out_shape=(jax.ShapeDtypeStruct((batch, dil, sub, A_WIDTH), BF16),
                   jax.ShapeDtypeStruct((batch, dil, sub, A_WIDTH), F32)),
        scratch_shapes=[pltpu.VMEM((n_res, step + 2 * A_HALF, A_WIDTH), BF16),
                        pltpu.VMEM((n_res, step + 2 * A_HALF, A_WIDTH), BF16)],
        compiler_params=pltpu.CompilerParams(
            dimension_semantics=("parallel", "parallel", "parallel"), vmem_limit_bytes=VMEM_LIMIT),
        name=f"mixer_a{gi}",
    )(qkv, qkv, qkv, bias, bias, bias)


def _b_bias():
    slopes = _alibi_slopes(B_HEADS).astype(np.float64) * LOG2E
    rows = np.arange(B_TQ + 2 * B_HALF)[:, None]
    rel = (rows - B_HALF) - np.arange(B_TQ)[None, :]
    bias = -slopes[:, None, None] * np.abs(rel)[None]
    out = []
    for variant in range(4):
        ok = np.abs(rel) <= B_HALF
        if variant & 1:
            ok = ok & (rows >= B_HALF)
        if variant & 2:
            ok = ok & (rows < B_HALF + B_TQ)
        out.append(np.where(ok[None], bias, NEG_INF))
    return np.stack(out).astype(np.float32)


def _b_kernel(sink_ref, q_ref, kp_ref, kc_ref, kn_ref, vp_ref, vc_ref, vn_ref, bias_ref,
              o_ref, kfull, vfull, s_even, s_odd):
    tk = B_TQ + 2 * B_HALF
    n_units = tk // B_UNIT
    kfull[0:B_HALF] = kp_ref[...]
    kfull[B_HALF:B_HALF + B_TQ] = kc_ref[...]
    kfull[B_HALF + B_TQ:] = kn_ref[...]
    vfull[:, 0:B_HALF] = vp_ref[0]
    for c in range(B_TQ // B_HALF):
        vfull[:, (c + 1) * B_HALF:(c + 2) * B_HALF] = vc_ref[c]
    vfull[:, B_HALF + B_TQ:] = vn_ref[0]
    zeros = jnp.zeros((HEAD_DIM, B_TQ), BF16)
    ones_rows = (lax.broadcasted_iota(jnp.int32, (C_ONES_ROWS, B_UNIT), 0) == 0).astype(BF16)
    neg = jnp.full((1, B_TQ), NEG_INF, F32)
    grp = B_HEADS // B_KV_HEADS

    def score_unit(h, u, s_scr, bm):
        r0 = u * B_UNIT
        qh = q_ref[0, h * HEAD_DIM:(h + 1) * HEAD_DIM, :]
        rhs = jnp.concatenate([qh, zeros] if h // grp == 0 else [zeros, qh], axis=0)
        s = (jnp.dot(kfull[r0:r0 + B_UNIT, :], rhs, preferred_element_type=F32)
             + bias_ref[0, h, r0:r0 + B_UNIT, :])
        s_scr[r0:r0 + B_UNIT, :] = s
        return jnp.maximum(bm, jnp.max(s, axis=0, keepdims=True))

    def value_unit(h, u, s_scr, m, pv):
        r0 = u * B_UNIT
        kvh = h // grp
        v = jnp.concatenate([vfull[kvh * HEAD_DIM:(kvh + 1) * HEAD_DIM, r0:r0 + B_UNIT], ones_rows],
                            axis=0)
        e = jnp.exp2(s_scr[r0:r0 + B_UNIT, :] - m)
        out = jnp.dot(v, e.astype(BF16), preferred_element_type=F32)
        return out if pv is None else pv + out

    scr = (s_even, s_odd)
    bm = neg
    for u in range(n_units):
        bm = score_unit(0, u, s_even, bm)
    outs = []
    for h in range(B_HEADS):
        sink = sink_ref[h]
        m = jnp.maximum(bm, sink)
        bm = neg
        pv = None
        for u in range(n_units):
            if h + 1 < B_HEADS:
                bm = score_unit(h + 1, u, scr[(h + 1) % 2], bm)
            pv = value_unit(h, u, scr[h % 2], m, pv)
        l = pv[HEAD_DIM:HEAD_DIM + 1] + jnp.exp2(sink - m)
        outs.append(pv[:HEAD_DIM] / l)
    o_ref[...] = jnp.concatenate(outs, axis=0).T.astype(o_ref.dtype)


def _b_call(sink2, qbt, kb, vbt, batch, seq):
    t = batch * seq
    n_q = seq // B_TQ
    nkb = seq // B_HALF
    r = B_TQ // B_HALF
    kvw = B_KV_HEADS * HEAD_DIM
    tk = B_TQ + 2 * B_HALF
    prev_i = lambda b, i: b * nkb + jnp.maximum(i * r - 1, 0)
    next_i = lambda b, i: b * nkb + jnp.minimum((i + 1) * r, nkb - 1)
    variant = lambda b, i: ((i == 0).astype(jnp.int32) + 2 * (i == n_q - 1).astype(jnp.int32), 0, 0, 0)
    bias = jnp.asarray(_b_bias())
    return pl.pallas_call(
        _b_kernel,
        grid=(batch, n_q),
        in_specs=[pl.BlockSpec(memory_space=pltpu.SMEM),
                  pl.BlockSpec((1, B_WIDTH, B_TQ), lambda b, i: (b * n_q + i, 0, 0)),
                  pl.BlockSpec((B_HALF, kvw), lambda b, i: (prev_i(b, i), 0)),
                  pl.BlockSpec((B_TQ, kvw), lambda b, i: (b * n_q + i, 0)),
                  pl.BlockSpec((B_HALF, kvw), lambda b, i: (next_i(b, i), 0)),
                  pl.BlockSpec((1, kvw, B_HALF), lambda b, i: (prev_i(b, i), 0, 0)),
                  pl.BlockSpec((r, kvw, B_HALF), lambda b, i: (b * n_q + i, 0, 0)),
                  pl.BlockSpec((1, kvw, B_HALF), lambda b, i: (next_i(b, i), 0, 0)),
                  pl.BlockSpec((1,) + bias.shape[1:], variant)],
        out_specs=pl.BlockSpec((B_TQ, B_WIDTH), lambda b, i: (b * n_q + i, 0)),
        out_shape=jax.ShapeDtypeStruct((t, B_WIDTH), BF16),
        scratch_shapes=[pltpu.VMEM((tk, kvw), BF16), pltpu.VMEM((kvw, tk), BF16),
                        pltpu.VMEM((tk, B_TQ), F32), pltpu.VMEM((tk, B_TQ), F32)],
        compiler_params=pltpu.CompilerParams(dimension_semantics=("parallel", "parallel"),
                                             vmem_limit_bytes=VMEM_LIMIT),
        name="mixer_b",
    )(sink2, qbt, kb, kb, kb, vbt, vbt, vbt, bias)


def _c_bias():
    slopes = _alibi_slopes(C_HEADS).astype(np.float64) * LOG2E
    d = (np.arange(C_TK)[:, None] - np.arange(C_TQ)[None, :]).astype(np.float64)
    tiles = [d, -d]
    for part in range(C_TK // C_TQ):
        tiles.append(-np.abs(d - part * C_TQ))
    return (slopes[:, None, None, None] * np.stack(tiles)[None]).astype(np.float32)


def _c_kernel(scal_ref, lamv_ref, q_ref, k_ref, v_ref, bias_ref, g_ref, o_ref,
              s_even, s_odd, acc, *, n_q, n_kv, stages):
    h = pl.program_id(1)
    ratio = C_TK // C_TQ
    n_units = C_TK // C_UNIT
    slope2 = scal_ref[h]
    lam_init = scal_ref[C_HEADS]
    zeros = jnp.zeros((HEAD_DIM, C_TQ), BF16)
    ones_rows = (lax.broadcasted_iota(jnp.int32, (C_ONES_ROWS, C_UNIT), 0) == 0).astype(BF16)
    neg = jnp.full((1, C_TQ), NEG_INF, F32)
    lv = lamv_ref[...]
    lam = (jnp.exp(jnp.sum(lv[0:1] * lv[1:2], axis=1, keepdims=True))
           - jnp.exp(jnp.sum(lv[2:3] * lv[3:4], axis=1, keepdims=True)) + lam_init)
    acc[...] = jnp.ones_like(acc)

    def block_offset(i, j):
        return jnp.where(j == i // ratio, 0.0, -slope2 * jnp.abs(C_TQ * i - C_TK * j).astype(F32))

    def score_units(i, j, s_scr):
        jd = i // ratio
        tile = jnp.where(j < jd, 0, jnp.where(j > jd, 1, 2 + i % ratio))
        q = q_ref[i]
        rhs = (jnp.concatenate([q[:HEAD_DIM], zeros], axis=0),
               jnp.concatenate([zeros, q[HEAD_DIM:]], axis=0))
        bm = [neg, neg]
        for u in range(n_units):
            r0 = u * C_UNIT
            k = k_ref[pl.ds(pl.multiple_of(j * C_TK + r0, C_UNIT), C_UNIT), :]
            bias = bias_ref[0, tile, r0:r0 + C_UNIT, :]
            for mp in range(2):
                s = jnp.dot(k, rhs[mp], preferred_element_type=F32) + bias
                s_scr[mp, r0:r0 + C_UNIT, :] = s
                bm[mp] = jnp.maximum(bm[mp], jnp.max(s, axis=0, keepdims=True))
            yield tuple(bm)

    def value_unit(j, u, s_scr, mn, pvs):
        r0 = u * C_UNIT
        v = jnp.concatenate([v_ref[j, :, r0:r0 + C_UNIT], ones_rows], axis=0)
        out = []
        for mp in range(2):
            e = jnp.exp2(s_scr[mp, r0:r0 + C_UNIT, :] - mn[mp])
            pv = jnp.dot(v, e.astype(BF16), preferred_element_type=F32)
            out.append(pv if pvs[mp] is None else pvs[mp] + pv)
        return tuple(out)

    def stage(aset, i, j, s_cur, bm_cur, m, i_next, j_next, s_next):
        c = block_offset(i, j)
        mn, alpha, m_new = [], [], []
        for mp in range(2):
            msh = m[mp] - c
            x = jnp.maximum(msh, bm_cur[mp])
            mn.append(x)
            alpha.append(jnp.exp2(msh - x))
            m_new.append(x + c)
        bm_next = (neg, neg)
        pvs = (None, None)
        nxt = score_units(i_next, j_next, s_next)
        for u in range(n_units):
            bm_next = next(nxt)
            pvs = value_unit(j, u, s_cur, mn, pvs)
        for mp in range(2):
            acc[aset, mp] = alpha[mp] * acc[aset, mp] + pvs[mp]
        return bm_next, tuple(m_new)

    def run_stages(aset, i, j0, count, bm, m, last_next, after_first=None):
        for d in range(count):
            j = j0 + d
            cur, oth = (s_even, s_odd) if d % 2 == 0 else (s_odd, s_even)
            i_n, j_n = (i, j + 1) if d + 1 < count or last_next is None else last_next
            bm, m = stage(aset, i, j, cur, bm, m, i_n, j_n, oth)
            if d == 0 and after_first is not None:
                after_first()
        return bm, m

    def finalize(aset, i):
        a1 = acc[aset, 0]
        a2 = acc[aset, 1]
        a = a1[:C_VDIM] / a1[C_VDIM:C_VDIM + 1] - lam * (a2[:C_VDIM] / a2[C_VDIM:C_VDIM + 1])
        ms = jnp.mean(a * a, axis=0, keepdims=True)
        y = a * lax.rsqrt(ms + RMS_EPS) * g_ref[...] * (1.0 - lam_init)
        o_ref[pl.ds(pl.multiple_of(i * C_TQ, C_TQ), C_TQ), :] = y.T.astype(o_ref.dtype)

    def q_block(aset, i, bm, fin_i):
        m = (neg, neg)
        after_first = lambda: finalize(1 - aset, fin_i)
        trips = n_kv // stages - 1
        if trips > 0:
            bm, m = run_stages(aset, i, 0, stages, bm, m, None, after_first)
            after_first = None
            if trips > 1:
                def trip(t, carry):
                    return run_stages(aset, i, t * stages, stages, carry[0], carry[1], None)
                bm, m = lax.fori_loop(1, trips, trip, (bm, m))
        i_next = jnp.minimum(i + 1, n_q - 1)
        bm, _ = run_stages(aset, i, n_kv - stages, stages, bm, m, (i_next, 0), after_first)
        return bm

    def q_pair(ip, bm):
        i0 = 2 * ip
        bm = q_block(0, i0, bm, jnp.maximum(i0 - 1, 0))
        return q_block(1, i0 + 1, bm, i0)

    bm0 = (neg, neg)
    for bm0 in score_units(0, 0, s_even):
        pass
    lax.fori_loop(0, n_q // 2, q_pair, bm0)
    finalize(1, n_q - 1)


def _c_call(scal, lamv, subln_g, qct, kc, vct, batch, seq):
    t = batch * seq
    n_q = seq // C_TQ
    n_kv = seq // C_TK
    kw = 2 * HEAD_DIM
    stages = min(C_STAGES, n_kv)
    assert n_kv % stages == 0 and stages % 2 == 0 and n_q % 2 == 0
    bias = jnp.asarray(_c_bias())
    g = jnp.broadcast_to(subln_g.astype(F32)[:, None], (C_VDIM, C_TQ))
    return pl.pallas_call(
        functools.partial(_c_kernel, n_q=n_q, n_kv=n_kv, stages=stages),
        grid=(batch, C_HEADS),
        in_specs=[pl.BlockSpec(memory_space=pltpu.SMEM),
                  _const_spec((4, HEAD_DIM)),
                  pl.BlockSpec((n_q, kw, C_TQ), lambda b, h: (b, h, 0)),
                  pl.BlockSpec((seq, kw), lambda b, h: (b, h)),
                  pl.BlockSpec((n_kv, C_VDIM, C_TK), lambda b, h: (b, h, 0)),
                  pl.BlockSpec((1,) + bias.shape[1:], lambda b, h: (h, 0, 0, 0)),
                  _const_spec((C_VDIM, C_TQ))],
        out_specs=pl.BlockSpec((seq, C_VDIM), lambda b, h: (b, h)),
        out_shape=jax.ShapeDtypeStruct((t, C_WIDTH), BF16),
        scratch_shapes=[pltpu.VMEM((2, C_TK, C_TQ), F32), pltpu.VMEM((2, C_TK, C_TQ), F32),
                        pltpu.VMEM((2, 2, C_VDIM + C_ONES_ROWS, C_TQ), F32)],
        compiler_params=pltpu.CompilerParams(
            dimension_semantics=("parallel", "parallel"), vmem_limit_bytes=VMEM_LIMIT),
        name="mixer_c",
    )(scal, lamv, qct, kc, vct, bias, g)


def _sigmoid(x):
    return 1.0 / (1.0 + jnp.exp(-x))


def _post_kernel(x_ref, g_ref, oa0_ref, oa1_ref, oa2_ref, la0_ref, la1_ref, la2_ref, ob_ref, oc_ref,
                 wg_ref, woa_ref, wob_ref, woc_ref, wout_ref, fg_ref, y_ref, tok_scr, *, final):
    x = x_ref[...]
    h = _rms(x, g_ref[...]).astype(BF16)

    def token_order(ref, slot):
        _, dil, n, width = ref.shape
        if dil == 1:
            return ref[0, 0]
        n_chunks = width // LANES
        for r in range(dil):
            rows = ref[0, r].astype(F32)
            for c in range(n_chunks):
                tok_scr[slot, c, pl.ds(r, n, stride=dil), :] = rows[:, c * LANES:(c + 1) * LANES]
        return jnp.concatenate([tok_scr[slot, c] for c in range(n_chunks)], axis=1)

    def gate(c0, n):
        return jnp.dot(h, wg_ref[:, c0:c0 + n], preferred_element_type=F32)

    def silu_gated(o, ga):
        return (o * (ga * _sigmoid(ga))).astype(BF16)

    g0 = A_WIDTH + B_WIDTH + C_WIDTH
    ga, gb, gc = gate(0, A_WIDTH), gate(A_WIDTH, B_WIDTH), gate(A_WIDTH + B_WIDTH, C_WIDTH)
    merge = [_sigmoid(gate(g0 + b * D_MODEL, D_MODEL)) for b in range(3)]

    l0, l1, l2 = token_order(la0_ref, 0), token_order(la1_ref, 0), token_order(la2_ref, 1)
    o0, o1, o2 = token_order(oa0_ref, 0), token_order(oa1_ref, 2), token_order(oa2_ref, 3)
    m = jnp.maximum(jnp.maximum(l0, l1), l2)
    e0, e1, e2 = jnp.exp2(l0 - m), jnp.exp2(l1 - m), jnp.exp2(l2 - m)
    oa = (e0 * o0 + e1 * o1 + e2 * o2) / (e0 + e1 + e2)

    ub, uc, ua = silu_gated(ob_ref[...], gb), silu_gated(oc_ref[...], gc), silu_gated(oa, ga)
    yb = jnp.dot(ub, wob_ref[...], preferred_element_type=F32)
    yc = jnp.dot(uc, woc_ref[...], preferred_element_type=F32)
    ya = jnp.dot(ua, woa_ref[...], preferred_element_type=F32)
    mixed = merge[0] * ya + merge[1] * yb + merge[2] * yc
    y = x + jnp.dot(mixed.astype(BF16), wout_ref[...], preferred_element_type=F32)
    if final:
        y = _rms(y, fg_ref[...])
    y_ref[...] = y


def _post_call(x, g, oa, la, ob, oc, w, final_g, final, seq):
    t = x.shape[0]
    tm = POST_TM
    nt = seq // tm
    tok = lambda n: pl.BlockSpec((tm, n), lambda i: (i, 0))
    res = [pl.BlockSpec((1, dil, tm // dil, A_WIDTH), lambda i: (i // nt, 0, i % nt, 0))
           for _, dil in A_PATTERNS]
    return pl.pallas_call(
        functools.partial(_post_kernel, final=final),
        grid=(t // tm,),
        in_specs=[tok(D_MODEL), _const_spec((1, D_MODEL))] + res + res + [tok(B_WIDTH), tok(C_WIDTH)]
                 + [_const_spec(w["g"].shape), _const_spec(w["oa"].shape), _const_spec(w["ob"].shape),
                    _const_spec(w["oc"].shape), _const_spec(w["out"].shape), _const_spec((1, D_MODEL))],
        out_specs=tok(D_MODEL),
        out_shape=jax.ShapeDtypeStruct((t, D_MODEL), F32),
        scratch_shapes=[pltpu.VMEM((4, A_WIDTH // LANES, tm, LANES), F32)],
        compiler_params=pltpu.CompilerParams(dimension_semantics=("parallel",),
                                             vmem_limit_bytes=VMEM_LIMIT),
        name="post_final" if final else "post",
    )(x, g, oa[0], oa[1], oa[2], la[0], la[1], la[2], ob, oc,
      w["g"], w["oa"], w["ob"], w["oc"], w["out"], final_g)


def _layer_weights(w_in, w_oa, w_ob, w_oc, w_out):
    col = lambda idx: w_in[:, IN_OFFSETS[idx]:IN_OFFSETS[idx + 1]]
    scale = HEAD_DIM ** -0.5
    ft = jnp.concatenate([col(4) * (scale * LOG2E), col(6), col(8) * (scale * LOG2E), col(10)], axis=1).T
    grp = lambda idx, gi: col(idx)[:, gi * A_WIDTH:(gi + 1) * A_WIDTH]
    a_qkv = lambda gi: jnp.concatenate([grp(0, gi) * (scale * LOG2E), grp(1, gi), grp(2, gi)],
                                       axis=1).astype(BF16)
    return {
        "a0": a_qkv(0), "a1": a_qkv(1), "a2": a_qkv(2),
        "kb": col(5).astype(BF16), "kc": col(9).astype(BF16), "ft": ft.astype(BF16),
        "g": jnp.concatenate([col(3), col(7), col(11), col(12)], axis=1).astype(BF16),
        "oa": w_oa.astype(BF16), "ob": w_ob.astype(BF16), "oc": w_oc.astype(BF16),
        "out": w_out.astype(BF16),
    }


def _trunk(x3, layers, final_g):
    batch, seq, _ = x3.shape
    x = x3.reshape(batch * seq, D_MODEL)
    fg = final_g.astype(F32).reshape(1, D_MODEL)
    for li, lw in enumerate(layers):
        w = lw["w"]
        a0, a1, a2, kb, kc, qbt, vbt, qct, vct = _proj_call(x, lw["norm_g"], w, batch, seq)
        oa, la = zip(*[_a_call(qkv, gi) for gi, qkv in enumerate((a0, a1, a2))])
        ob = _b_call(lw["sink"], qbt, kb, vbt, batch, seq)
        oc = _c_call(lw["scal"], lw["lamv"], lw["subln_g"], qct, kc, vct, batch, seq)
        x = _post_call(x, lw["norm_g"], oa, la, ob, oc, w, fg, final=(li == len(layers) - 1), seq=seq)
    return x.reshape(batch, seq, D_MODEL)


def _prepare_layers(norm_g, w_in, w_oa, w_ob, w_oc, w_out, b_sink, lam_q1, lam_k1, lam_q2, lam_k2, c_subln_g):
    c_slopes = jnp.asarray(_alibi_slopes(C_HEADS) * np.float32(LOG2E))
    layers = []
    for l in range(DEPTH):
        lam_init = 0.8 - 0.6 * math.exp(-0.3 * l)
        layers.append({
            "w": _layer_weights(w_in[l], w_oa[l], w_ob[l], w_oc[l], w_out[l]),
            "norm_g": norm_g[l].astype(F32).reshape(1, D_MODEL),
            "sink": b_sink[l].astype(F32) * LOG2E,
            "scal": jnp.concatenate([c_slopes, jnp.full((1,), lam_init, F32)]),
            "lamv": jnp.stack([lam_q1[l], lam_k1[l], lam_q2[l], lam_k2[l]]).astype(F32),
            "subln_g": c_subln_g[l],
        })
    return layers


def kernel(x_prompt, x_sample, norm_g, w_in, w_oa, w_ob, w_oc, w_out, b_sink, lam_q1, lam_k1, lam_q2, lam_k2, c_subln_g, final_norm_g):
    layers = _prepare_layers(norm_g, w_in, w_oa, w_ob, w_oc, w_out, b_sink,
                             lam_q1, lam_k1, lam_q2, lam_k2, c_subln_g)
    return (_trunk(x_prompt, layers, final_norm_g), _trunk(x_sample, layers, final_norm_g))
```

```python
import functools
import math

import numpy as np
import jax
import jax.numpy as jnp
from jax import lax
from jax.experimental import pallas as pl
from jax.experimental.pallas import tpu as pltpu

F32 = jnp.float32
BF16 = jnp.bfloat16

D_MODEL = 1024
DEPTH = 4
HEAD_DIM = 64
A_PATTERNS = ((128, 1), (512, 4), (2048, 16))
A_GROUPS = 3
A_HEADS = 8
A_WIDTH = A_HEADS * HEAD_DIM
A_HALF = 64
B_HEADS = 8
B_KV_HEADS = 2
B_HALF = 128
B_WIDTH = B_HEADS * HEAD_DIM
C_HEADS = 4
C_VDIM = 2 * HEAD_DIM
C_WIDTH = C_HEADS * C_VDIM
RMS_EPS = 1e-6
NEG_INF = -1e30
IN_SIZES = (
    A_GROUPS * A_WIDTH, A_GROUPS * A_WIDTH, A_GROUPS * A_WIDTH, A_WIDTH,
    B_WIDTH, B_KV_HEADS * HEAD_DIM, B_KV_HEADS * HEAD_DIM, B_WIDTH,
    2 * C_HEADS * HEAD_DIM, 2 * C_HEADS * HEAD_DIM, C_WIDTH, C_WIDTH,
    3 * D_MODEL,
)
IN_OFFSETS = tuple(int(c) for c in np.cumsum((0,) + IN_SIZES))

PROJ_TM = 512
POST_TM = 512
A_TQ = 128
A_STEP = 1024
B_TQ = 256
B_UNIT = 256
B_CHAINS = 2
C_TQ = 256
C_TK = 512
C_UNIT = 256
C_STAGES = 8
C_ONES_ROWS = 16
LOG2E = 1.4426950408889634
LANES = 128
VMEM_LIMIT = 56 * 2**20

N_FEAT = B_WIDTH + B_KV_HEADS * HEAD_DIM + 2 * C_HEADS * HEAD_DIM + C_WIDTH


def _alibi_slopes(n):
    return np.asarray([2.0 ** (-8.0 * (i + 1) / n) for i in range(n)], dtype=np.float32)


def _rms(x, g):
    ms = jnp.mean(x * x, axis=-1, keepdims=True)
    return x * lax.rsqrt(ms + RMS_EPS) * g


def _const_spec(shape):
    nd = len(shape)
    return pl.BlockSpec(shape, lambda *_: (0,) * nd, pipeline_mode=pl.Buffered(1))


def _proj_kernel(x_ref, g_ref, wa0_ref, wa1_ref, wa2_ref, wkb_ref, wkc_ref, wft_ref,
                 a0_ref, a1_ref, a2_ref, kb_ref, kc_ref, qbt_ref, vbt_ref, qct_ref, vct_ref, h_scr):
    tm = x_ref.shape[0]
    h32 = _rms(x_ref[...], g_ref[...])
    h = h32.astype(BF16)
    a0_ref[0, 0] = jnp.dot(h, wa0_ref[...], preferred_element_type=F32).astype(BF16)
    n_chunks = h_scr.shape[0]
    for c in range(n_chunks):
        h_scr[c] = h32[:, c * LANES:(c + 1) * LANES]
    for w_ref, o_ref, (_, dil) in ((wa1_ref, a1_ref, A_PATTERNS[1]), (wa2_ref, a2_ref, A_PATTERNS[2])):
        n = tm // dil
        hp = jnp.concatenate(
            [jnp.concatenate([h_scr[c, pl.ds(r, n, stride=dil), :] for c in range(n_chunks)], axis=1)
             for r in range(dil)], axis=0).astype(BF16)
        res = jnp.dot(hp, w_ref[...], preferred_element_type=F32).astype(BF16)
        for r in range(dil):
            o_ref[0, r] = res[r * n:(r + 1) * n]
    for w_ref, o_ref in ((wkb_ref, kb_ref), (wkc_ref, kc_ref)):
        o_ref[...] = jnp.dot(h, w_ref[...], preferred_element_type=F32).astype(BF16)
    ft = lax.dot_general(wft_ref[...], h, (((1,), (1,)), ((), ())),
                         preferred_element_type=F32).astype(BF16)
    r0 = 0
    for o_ref, rows in ((qbt_ref, B_WIDTH), (vbt_ref, B_KV_HEADS * HEAD_DIM),
                        (qct_ref, 2 * C_HEADS * HEAD_DIM), (vct_ref, C_WIDTH)):
        nblk, _, width = o_ref.shape
        for c in range(nblk):
            o_ref[c] = ft[r0:r0 + rows, c * width:(c + 1) * width]
        r0 += rows


def _proj_call(x, g, w, batch, seq):
    t = x.shape[0]
    tm = PROJ_TM
    nt = seq // tm
    tok = lambda n: pl.BlockSpec((tm, n), lambda i: (i, 0))
    feat = lambda rows, width: pl.BlockSpec((tm // width, rows, width), lambda i: (i, 0, 0))
    a_cols = 3 * A_WIDTH
    res = lambda dil: pl.BlockSpec((1, dil, tm // dil, a_cols), lambda i: (i // nt, 0, i % nt, 0))
    a_shape = lambda dil: jax.ShapeDtypeStruct((batch, dil, seq // dil, a_cols), BF16)
    dils = [d for _, d in A_PATTERNS]
    out_shape = (
        a_shape(dils[0]), a_shape(dils[1]), a_shape(dils[2]),
        jax.ShapeDtypeStruct((t, B_KV_HEADS * HEAD_DIM), BF16),
        jax.ShapeDtypeStruct((t, 2 * C_HEADS * HEAD_DIM), BF16),
        jax.ShapeDtypeStruct((t // B_TQ, B_WIDTH, B_TQ), BF16),
        jax.ShapeDtypeStruct((t // B_HALF, B_KV_HEADS * HEAD_DIM, B_HALF), BF16),
        jax.ShapeDtypeStruct((t // C_TQ, 2 * C_HEADS * HEAD_DIM, C_TQ), BF16),
        jax.ShapeDtypeStruct((t // C_TK, C_WIDTH, C_TK), BF16),
    )
    return pl.pallas_call(
        _proj_kernel,
        grid=(t // tm,),
        in_specs=[tok(D_MODEL), _const_spec((1, D_MODEL)),
                  _const_spec(w["a0"].shape), _const_spec(w["a1"].shape), _const_spec(w["a2"].shape),
                  _const_spec(w["kb"].shape), _const_spec(w["kc"].shape), _const_spec(w["ft"].shape)],
        out_specs=(res(dils[0]), res(dils[1]), res(dils[2]), tok(B_KV_HEADS * HEAD_DIM),
                   tok(2 * C_HEADS * HEAD_DIM),
                   feat(B_WIDTH, B_TQ), feat(B_KV_HEADS * HEAD_DIM, B_HALF),
                   feat(2 * C_HEADS * HEAD_DIM, C_TQ), feat(C_WIDTH, C_TK)),
        out_shape=out_shape,
        scratch_shapes=[pltpu.VMEM((D_MODEL // LANES, tm, LANES), F32)],
        compiler_params=pltpu.CompilerParams(dimension_semantics=("parallel",),
                                             vmem_limit_bytes=VMEM_LIMIT),
        name="proj",
    )(x, g, w["a0"], w["a1"], w["a2"], w["kb"], w["kc"], w["ft"])


def _a_bias(gi):
    dil = A_PATTERNS[gi][1]
    slopes = _alibi_slopes(A_GROUPS * A_HEADS).reshape(A_GROUPS, A_HEADS)[gi].astype(np.float64) * LOG2E
    cols = np.arange(A_TQ + 2 * A_HALF)[None, :]
    rel = (cols - A_HALF) - np.arange(A_TQ)[:, None]
    bias = -slopes[:, None, None] * (np.abs(rel) * float(dil))[None]
    out = []
    for variant in range(4):
        ok = np.abs(rel) <= A_HALF
        if variant & 1:
            ok = ok & (cols >= A_HALF)
        if variant & 2:
            ok = ok & (cols < A_TQ + A_HALF)
        out.append(np.where(ok[None], bias, NEG_INF))
    return np.stack(out).astype(np.float32)


def _a_kernel(cur_ref, prev_ref, next_ref, bias_first_ref, bias_mid_ref, bias_last_ref,
              o_ref, lse_ref, kfull, vfull, *, step, n_res):
    tkw = A_TQ + 2 * A_HALF
    for rr in range(n_res):
        for full, k0 in ((kfull, A_WIDTH), (vfull, 2 * A_WIDTH)):
            full[rr, 0:A_HALF] = prev_ref[0, rr, :, k0:k0 + A_WIDTH]
            full[rr, A_HALF:A_HALF + step] = cur_ref[0, rr, :, k0:k0 + A_WIDTH]
            full[rr, A_HALF + step:] = next_ref[0, rr, :, k0:k0 + A_WIDTH]
    n_sb = step // A_TQ
    lane = lax.broadcasted_iota(jnp.int32, (tkw, 2 * HEAD_DIM), 1)
    lane_o = lax.broadcasted_iota(jnp.int32, (A_TQ, 2 * HEAD_DIM), 1)
    ones_cols = jnp.ones((tkw, 2 * HEAD_DIM), BF16)
    tasks = [(rr, sb, p, hh) for rr in range(n_res) for sb in range(n_sb)
             for p in range(A_HEADS // 2) for hh in range(2)]

    def in_half(hh):
        return (lane >= hh * HEAD_DIM) & (lane < (hh + 1) * HEAD_DIM)

    def scores(task):
        rr, sb, p, hh = task
        r0, c0 = sb * A_TQ, p * 2 * HEAD_DIM
        bias_ref = bias_first_ref if sb == 0 else (bias_last_ref if sb == n_sb - 1 else bias_mid_ref)
        qp = cur_ref[0, rr, r0:r0 + A_TQ, c0:c0 + 2 * HEAD_DIM]
        kw = kfull[rr, r0:r0 + tkw, c0:c0 + 2 * HEAD_DIM]
        kz = jnp.where(in_half(hh), kw, jnp.zeros_like(kw))
        s = lax.dot_general(qp, kz, (((1,), (1,)), ((), ())), preferred_element_type=F32)
        s = s + bias_ref[0, 2 * p + hh]
        return s, jnp.max(s, axis=1, keepdims=True)

    def values(task, s, m):
        rr, sb, p, hh = task
        r0, c0 = sb * A_TQ, p * 2 * HEAD_DIM
        vw = vfull[rr, r0:r0 + tkw, c0:c0 + 2 * HEAD_DIM]
        vz = jnp.concatenate([jnp.where(in_half(hh), vw, jnp.zeros_like(vw)), ones_cols], axis=1)
        e = jnp.exp2(s - m)
        res = jnp.dot(e.astype(BF16), vz, preferred_element_type=F32)
        l = res[:, 2 * HEAD_DIM:]
        return res[:, :2 * HEAD_DIM] / l, m + jnp.log(l) * LOG2E

    nxt = scores(tasks[0])
    o_pair = lse_pair = None
    for ti, task in enumerate(tasks):
        s, m = nxt
        if ti + 1 < len(tasks):
            nxt = scores(tasks[ti + 1])
        o, lse = values(task, s, m)
        rr, sb, p, hh = task
        if hh == 0:
            o_pair, lse_pair = o, lse
        else:
            r0, c0 = sb * A_TQ, p * 2 * HEAD_DIM
            o_ref[0, rr, r0:r0 + A_TQ, c0:c0 + 2 * HEAD_DIM] = (o_pair + o).astype(o_ref.dtype)
            lse_ref[0, rr, r0:r0 + A_TQ, c0:c0 + 2 * HEAD_DIM] = jnp.where(lane_o < HEAD_DIM, lse_pair, lse)


def _a_call(qkv, gi):
    batch, dil, sub, cols = qkv.shape
    step = min(A_STEP, sub)
    n_res = min(dil, A_STEP // step)
    n_steps = sub // step
    hb = step // A_HALF
    nhb = sub // A_HALF
    cur = lambda width: pl.BlockSpec((1, n_res, step, width), lambda b, r, u: (b, r, u, 0))
    prev = pl.BlockSpec((1, n_res, A_HALF, cols), lambda b, r, u: (b, r, jnp.maximum(u * hb - 1, 0), 0))
    nxt = pl.BlockSpec((1, n_res, A_HALF, cols),
                       lambda b, r, u: (b, r, jnp.minimum((u + 1) * hb, nhb - 1), 0))
    bias = jnp.asarray(_a_bias(gi))
    at_start = lambda u: (u == 0).astype(jnp.int32)
    at_end = lambda u: 2 * (u == n_steps - 1).astype(jnp.int32)
    single = step == A_TQ
    bias_spec = lambda variant: pl.BlockSpec((1,) + bias.shape[1:], lambda b, r, u: (variant(u), 0, 0, 0))
    return pl.pallas_call(
        functools.partial(_a_kernel, step=step, n_res=n_res),
        grid=(batch, dil // n_res, n_steps),
        in_specs=[cur(cols), prev, nxt,
                  bias_spec(lambda u: at_start(u) + (at_end(u) if single else 0)),
                  bias_spec(lambda u: 0 * u), bias_spec(at_end)],
        out_specs=(cur(A_WIDTH), cur(A_WIDTH)),
        out_shape=(jax.ShapeDtypeStruct((batch, dil, sub, A_WIDTH), BF16),
                   jax.ShapeDtypeStruct((batch, dil, sub, A_WIDTH), F32)),
        scratch_shapes=[pltpu.VMEM((n_res, step + 2 * A_HALF, A_WIDTH), BF16),
                        pltpu.VMEM((n_res, step + 2 * A_HALF, A_WIDTH), BF16)],
        compiler_params=pltpu.CompilerParams(
            dimension_semantics=("parallel", "parallel", "parallel"), vmem_limit_bytes=VMEM_LIMIT),
        name=f"mixer_a{gi}",
    )(qkv, qkv, qkv, bias, bias, bias)


def _b_bias():
    slopes = _alibi_slopes(B_HEADS).astype(np.float64) * LOG2E
    rows = np.arange(B_TQ + 2 * B_HALF)[:, None]
    rel = (rows - B_HALF) - np.arange(B_TQ)[None, :]
    bias = -slopes[:, None, None] * np.abs(rel)[None]
    out = []
    for variant in range(4):
        ok = np.abs(rel) <= B_HALF
        if variant & 1:
            ok = ok & (rows >= B_HALF)
        if variant & 2:
            ok = ok & (rows < B_HALF + B_TQ)
        out.append(np.where(ok[None], bias, NEG_INF))
    return np.stack(out).astype(np.float32)


def _b_kernel(sink_ref, q_ref, kp_ref, kc_ref, kn_ref, vp_ref, vc_ref, vn_ref, bias_ref,
              o_ref, kfull, vfull, s_scr):
    tk = B_TQ + 2 * B_HALF
    n_units = tk // B_UNIT
    kfull[0:B_HALF] = kp_ref[...]
    kfull[B_HALF:B_HALF + B_TQ] = kc_ref[...]
    kfull[B_HALF + B_TQ:] = kn_ref[...]
    vfull[:, 0:B_HALF] = vp_ref[0]
    for c in range(B_TQ // B_HALF):
        vfull[:, (c + 1) * B_HALF:(c + 2) * B_HALF] = vc_ref[c]
    vfull[:, B_HALF + B_TQ:] = vn_ref[0]
    zeros = jnp.zeros((HEAD_DIM, B_TQ), BF16)
    ones_rows = (lax.broadcasted_iota(jnp.int32, (C_ONES_ROWS, B_UNIT), 0) == 0).astype(BF16)
    neg = jnp.full((1, B_TQ), NEG_INF, F32)
    grp = B_HEADS // B_KV_HEADS

    def score_unit(h, u, s_scr, bm):
        r0 = u * B_UNIT
        qh = q_ref[0, h * HEAD_DIM:(h + 1) * HEAD_DIM, :]
        rhs = jnp.concatenate([qh, zeros] if h // grp == 0 else [zeros, qh], axis=0)
        s = (jnp.dot(kfull[r0:r0 + B_UNIT, :], rhs, preferred_element_type=F32)
             + bias_ref[0, h, r0:r0 + B_UNIT, :])
        s_scr[r0:r0 + B_UNIT, :] = s
        return jnp.maximum(bm, jnp.max(s, axis=0, keepdims=True))

    def value_unit(h, u, s_scr, m, pv):
        r0 = u * B_UNIT
        kvh = h // grp
        v = jnp.concatenate([vfull[kvh * HEAD_DIM:(kvh + 1) * HEAD_DIM, r0:r0 + B_UNIT], ones_rows],
                            axis=0)
        e = jnp.exp2(s_scr[r0:r0 + B_UNIT, :] - m)
        out = jnp.dot(v, e.astype(BF16), preferred_element_type=F32)
        return out if pv is None else pv + out

    per_chain = B_HEADS // B_CHAINS
    scr = lambda c, t: s_scr.at[c, t % 2]
    bms = [neg] * B_CHAINS
    for u in range(n_units):
        bms = [score_unit(c, u, scr(c, 0), bms[c]) for c in range(B_CHAINS)]
    outs = [None] * B_HEADS
    for t in range(per_chain):
        heads = [c + t * B_CHAINS for c in range(B_CHAINS)]
        sinks = [sink_ref[h] for h in heads]
        ms = [jnp.maximum(bms[c], sinks[c]) for c in range(B_CHAINS)]
        bms = [neg] * B_CHAINS
        pvs = [None] * B_CHAINS
        for u in range(n_units):
            if t + 1 < per_chain:
                bms = [score_unit(heads[c] + B_CHAINS, u, scr(c, t + 1), bms[c]) for c in range(B_CHAINS)]
            pvs = [value_unit(heads[c], u, scr(c, t), ms[c], pvs[c]) for c in range(B_CHAINS)]
        for c, h in enumerate(heads):
            l = pvs[c][HEAD_DIM:HEAD_DIM + 1] + jnp.exp2(sinks[c] - ms[c])
            outs[h] = pvs[c][:HEAD_DIM] / l
    o_ref[...] = jnp.concatenate(outs, axis=0).T.astype(o_ref.dtype)


def _b_call(sink2, qbt, kb, vbt, batch, seq):
    t = batch * seq
    n_q = seq // B_TQ
    nkb = seq // B_HALF
    r = B_TQ // B_HALF
    kvw = B_KV_HEADS * HEAD_DIM
    tk = B_TQ + 2 * B_HALF
    prev_i = lambda b, i: b * nkb + jnp.maximum(i * r - 1, 0)
    next_i = lambda b, i: b * nkb + jnp.minimum((i + 1) * r, nkb - 1)
    variant = lambda b, i: ((i == 0).astype(jnp.int32) + 2 * (i == n_q - 1).astype(jnp.int32), 0, 0, 0)
    bias = jnp.asarray(_b_bias())
    return pl.pallas_call(
        _b_kernel,
        grid=(batch, n_q),
        in_specs=[pl.BlockSpec(memory_space=pltpu.SMEM),
                  pl.BlockSpec((1, B_WIDTH, B_TQ), lambda b, i: (b * n_q + i, 0, 0)),
                  pl.BlockSpec((B_HALF, kvw), lambda b, i: (prev_i(b, i), 0)),
                  pl.BlockSpec((B_TQ, kvw), lambda b, i: (b * n_q + i, 0)),
                  pl.BlockSpec((B_HALF, kvw), lambda b, i: (next_i(b, i), 0)),
                  pl.BlockSpec((1, kvw, B_HALF), lambda b, i: (prev_i(b, i), 0, 0)),
                  pl.BlockSpec((r, kvw, B_HALF), lambda b, i: (b * n_q + i, 0, 0)),
                  pl.BlockSpec((1, kvw, B_HALF), lambda b, i: (next_i(b, i), 0, 0)),
                  pl.BlockSpec((1,) + bias.shape[1:], variant)],
        out_specs=pl.BlockSpec((B_TQ, B_WIDTH), lambda b, i: (b * n_q + i, 0)),
        out_shape=jax.ShapeDtypeStruct((t, B_WIDTH), BF16),
        scratch_shapes=[pltpu.VMEM((tk, kvw), BF16), pltpu.VMEM((kvw, tk), BF16),
                        pltpu.VMEM((B_CHAINS, 2, tk, B_TQ), F32)],
        compiler_params=pltpu.CompilerParams(dimension_semantics=("parallel", "parallel"),
                                             vmem_limit_bytes=VMEM_LIMIT),
        name="mixer_b",
    )(sink2, qbt, kb, kb, kb, vbt, vbt, vbt, bias)


def _c_bias():
    slopes = _alibi_slopes(C_HEADS).astype(np.float64) * LOG2E
    d = (np.arange(C_TK)[:, None] - np.arange(C_TQ)[None, :]).astype(np.float64)
    tiles = [d, -d]
    for part in range(C_TK // C_TQ):
        tiles.append(-np.abs(d - part * C_TQ))
    return (slopes[:, None, None, None] * np.stack(tiles)[None]).astype(np.float32)


def _c_kernel(scal_ref, lamv_ref, q_ref, k_ref, v_ref, bias_ref, g_ref, o_ref,
              s_even, s_odd, acc, *, n_q, n_kv, stages):
    h = pl.program_id(1)
    ratio = C_TK // C_TQ
    n_units = C_TK // C_UNIT
    slope2 = scal_ref[h]
    lam_init = scal_ref[C_HEADS]
    zeros = jnp.zeros((HEAD_DIM, C_TQ), BF16)
    ones_rows = (lax.broadcasted_iota(jnp.int32, (C_ONES_ROWS, C_UNIT), 0) == 0).astype(BF16)
    neg = jnp.full((1, C_TQ), NEG_INF, F32)
    lv = lamv_ref[...]
    lam = (jnp.exp(jnp.sum(lv[0:1] * lv[1:2], axis=1, keepdims=True))
           - jnp.exp(jnp.sum(lv[2:3] * lv[3:4], axis=1, keepdims=True)) + lam_init)
    acc[...] = jnp.ones_like(acc)

    def block_offset(i, j):
        return jnp.where(j == i // ratio, 0.0, -slope2 * jnp.abs(C_TQ * i - C_TK * j).astype(F32))

    def score_units(i, j, s_scr):
        jd = i // ratio
        tile = jnp.where(j < jd, 0, jnp.where(j > jd, 1, 2 + i % ratio))
        q = q_ref[i]
        rhs = (jnp.concatenate([q[:HEAD_DIM], zeros], axis=0),
               jnp.concatenate([zeros, q[HEAD_DIM:]], axis=0))
        bm = [neg, neg]
        for u in range(n_units):
            r0 = u * C_UNIT
            k = k_ref[pl.ds(pl.multiple_of(j * C_TK + r0, C_UNIT), C_UNIT), :]
            bias = bias_ref[0, tile, r0:r0 + C_UNIT, :]
            for mp in range(2):
                s = jnp.dot(k, rhs[mp], preferred_element_type=F32) + bias
                s_scr[mp, r0:r0 + C_UNIT, :] = s
                bm[mp] = jnp.maximum(bm[mp], jnp.max(s, axis=0, keepdims=True))
            yield tuple(bm)

    def value_unit(j, u, s_scr, mn, pvs):
        r0 = u * C_UNIT
        v = jnp.concatenate([v_ref[j, :, r0:r0 + C_UNIT], ones_rows], axis=0)
        out = []
        for mp in range(2):
            e = jnp.exp2(s_scr[mp, r0:r0 + C_UNIT, :] - mn[mp])
            pv = jnp.dot(v, e.astype(BF16), preferred_element_type=F32)
            out.append(pv if pvs[mp] is None else pvs[mp] + pv)
        return tuple(out)

    def stage(aset, i, j, s_cur, bm_cur, m, i_next, j_next, s_next):
        c = block_offset(i, j)
        mn, alpha, m_new = [], [], []
        for mp in range(2):
            msh = m[mp] - c
            x = jnp.maximum(msh, bm_cur[mp])
            mn.append(x)
            alpha.append(jnp.exp2(msh - x))
            m_new.append(x + c)
        bm_next = (neg, neg)
        pvs = (None, None)
        nxt = score_units(i_next, j_next, s_next)
        for u in range(n_units):
            bm_next = next(nxt)
            pvs = value_unit(j, u, s_cur, mn, pvs)
        for mp in range(2):
            acc[aset, mp] = alpha[mp] * acc[aset, mp] + pvs[mp]
        return bm_next, tuple(m_new)

    def run_stages(aset, i, j0, count, bm, m, last_next, after_first=None):
        for d in range(count):
            j = j0 + d
            cur, oth = (s_even, s_odd) if d % 2 == 0 else (s_odd, s_even)
            i_n, j_n = (i, j + 1) if d + 1 < count or last_next is None else last_next
            bm, m = stage(aset, i, j, cur, bm, m, i_n, j_n, oth)
            if d == 0 and after_first is not None:
                after_first()
        return bm, m

    def finalize(aset, i):
        a1 = acc[aset, 0]
        a2 = acc[aset, 1]
        a = a1[:C_VDIM] / a1[C_VDIM:C_VDIM + 1] - lam * (a2[:C_VDIM] / a2[C_VDIM:C_VDIM + 1])
        ms = jnp.mean(a * a, axis=0, keepdims=True)
        y = a * lax.rsqrt(ms + RMS_EPS) * g_ref[...] * (1.0 - lam_init)
        o_ref[pl.ds(pl.multiple_of(i * C_TQ, C_TQ), C_TQ), :] = y.T.astype(o_ref.dtype)

    def q_block(aset, i, bm, fin_i):
        m = (neg, neg)
        after_first = lambda: finalize(1 - aset, fin_i)
        trips = n_kv // stages - 1
        if trips > 0:
            bm, m = run_stages(aset, i, 0, stages, bm, m, None, after_first)
            after_first = None
            if trips > 1:
                def trip(t, carry):
                    return run_stages(aset, i, t * stages, stages, carry[0], carry[1], None)
                bm, m = lax.fori_loop(1, trips, trip, (bm, m))
        i_next = jnp.minimum(i + 1, n_q - 1)
        bm, _ = run_stages(aset, i, n_kv - stages, stages, bm, m, (i_next, 0), after_first)
        return bm

    def q_pair(ip, bm):
        i0 = 2 * ip
        bm = q_block(0, i0, bm, jnp.maximum(i0 - 1, 0))
        return q_block(1, i0 + 1, bm, i0)

    bm0 = (neg, neg)
    for bm0 in score_units(0, 0, s_even):
        pass
    lax.fori_loop(0, n_q // 2, q_pair, bm0)
    finalize(1, n_q - 1)


def _c_call(scal, lamv, subln_g, qct, kc, vct, batch, seq):
    t = batch * seq
    n_q = seq // C_TQ
    n_kv = seq // C_TK
    kw = 2 * HEAD_DIM
    stages = min(C_STAGES, n_kv)
    assert n_kv % stages == 0 and stages % 2 == 0 and n_q % 2 == 0
    bias = jnp.asarray(_c_bias())
    g = jnp.broadcast_to(subln_g.astype(F32)[:, None], (C_VDIM, C_TQ))
    return pl.pallas_call(
        functools.partial(_c_kernel, n_q=n_q, n_kv=n_kv, stages=stages),
        grid=(batch, C_HEADS),
        in_specs=[pl.BlockSpec(memory_space=pltpu.SMEM),
                  _const_spec((4, HEAD_DIM)),
                  pl.BlockSpec((n_q, kw, C_TQ), lambda b, h: (b, h, 0)),
                  pl.BlockSpec((seq, kw), lambda b, h: (b, h)),
                  pl.BlockSpec((n_kv, C_VDIM, C_TK), lambda b, h: (b, h, 0)),
                  pl.BlockSpec((1,) + bias.shape[1:], lambda b, h: (h, 0, 0, 0)),
                  _const_spec((C_VDIM, C_TQ))],
        out_specs=pl.BlockSpec((seq, C_VDIM), lambda b, h: (b, h)),
        out_shape=jax.ShapeDtypeStruct((t, C_WIDTH), BF16),
        scratch_shapes=[pltpu.VMEM((2, C_TK, C_TQ), F32), pltpu.VMEM((2, C_TK, C_TQ), F32),
                        pltpu.VMEM((2, 2, C_VDIM + C_ONES_ROWS, C_TQ), F32)],
        compiler_params=pltpu.CompilerParams(
            dimension_semantics=("parallel", "parallel"), vmem_limit_bytes=VMEM_LIMIT),
        name="mixer_c",
    )(scal, lamv, qct, kc, vct, bias, g)


def _sigmoid(x):
    return 1.0 / (1.0 + jnp.exp(-x))


def _post_kernel(x_ref, g_ref, oa0_ref, oa1_ref, oa2_ref, la0_ref, la1_ref, la2_ref, ob_ref, oc_ref,
                 wg_ref, woa_ref, wob_ref, woc_ref, wout_ref, fg_ref, y_ref, tok_scr, *, final):
    x = x_ref[...]
    h = _rms(x, g_ref[...]).astype(BF16)

    def token_order(ref, slot):
        _, dil, n, width = ref.shape
        if dil == 1:
            return ref[0, 0]
        n_chunks = width // LANES
        for r in range(dil):
            rows = ref[0, r].astype(F32)
            for c in range(n_chunks):
                tok_scr[slot, c, pl.ds(r, n, stride=dil), :] = rows[:, c * LANES:(c + 1) * LANES]
        return jnp.concatenate([tok_scr[slot, c] for c in range(n_chunks)], axis=1)

    def gate(c0, n):
        return jnp.dot(h, wg_ref[:, c0:c0 + n], preferred_element_type=F32)

    def silu_gated(o, ga):
        return (o * (ga * _sigmoid(ga))).astype(BF16)

    g0 = A_WIDTH + B_WIDTH + C_WIDTH
    ga, gb, gc = gate(0, A_WIDTH), gate(A_WIDTH, B_WIDTH), gate(A_WIDTH + B_WIDTH, C_WIDTH)
    merge = [_sigmoid(gate(g0 + b * D_MODEL, D_MODEL)) for b in range(3)]

    l0, l1, l2 = token_order(la0_ref, 0), token_order(la1_ref, 0), token_order(la2_ref, 1)
    o0, o1, o2 = token_order(oa0_ref, 0), token_order(oa1_ref, 2), token_order(oa2_ref, 3)
    m = jnp.maximum(jnp.maximum(l0, l1), l2)
    e0, e1, e2 = jnp.exp2(l0 - m), jnp.exp2(l1 - m), jnp.exp2(l2 - m)
    oa = (e0 * o0 + e1 * o1 + e2 * o2) / (e0 + e1 + e2)

    ub, uc, ua = silu_gated(ob_ref[...], gb), silu_gated(oc_ref[...], gc), silu_gated(oa, ga)
    yb = jnp.dot(ub, wob_ref[...], preferred_element_type=F32)
    yc = jnp.dot(uc, woc_ref[...], preferred_element_type=F32)
    ya = jnp.dot(ua, woa_ref[...], preferred_element_type=F32)
    mixed = merge[0] * ya + merge[1] * yb + merge[2] * yc
    y = x + jnp.dot(mixed.astype(BF16), wout_ref[...], preferred_element_type=F32)
    if final:
        y = _rms(y, fg_ref[...])
    y_ref[...] = y


def _post_call(x, g, oa, la, ob, oc, w, final_g, final, seq):
    t = x.shape[0]
    tm = POST_TM
    nt = seq // tm
    tok = lambda n: pl.BlockSpec((tm, n), lambda i: (i, 0))
    res = [pl.BlockSpec((1, dil, tm // dil, A_WIDTH), lambda i: (i // nt, 0, i % nt, 0))
           for _, dil in A_PATTERNS]
    return pl.pallas_call(
        functools.partial(_post_kernel, final=final),
        grid=(t // tm,),
        in_specs=[tok(D_MODEL), _const_spec((1, D_MODEL))] + res + res + [tok(B_WIDTH), tok(C_WIDTH)]
                 + [_const_spec(w["g"].shape), _const_spec(w["oa"].shape), _const_spec(w["ob"].shape),
                    _const_spec(w["oc"].shape), _const_spec(w["out"].shape), _const_spec((1, D_MODEL))],
        out_specs=tok(D_MODEL),
        out_shape=jax.ShapeDtypeStruct((t, D_MODEL), F32),
        scratch_shapes=[pltpu.VMEM((4, A_WIDTH // LANES, tm, LANES), F32)],
        compiler_params=pltpu.CompilerParams(dimension_semantics=("parallel",),
                                             vmem_limit_bytes=VMEM_LIMIT),
        name="post_final" if final else "post",
    )(x, g, oa[0], oa[1], oa[2], la[0], la[1], la[2], ob, oc,
      w["g"], w["oa"], w["ob"], w["oc"], w["out"], final_g)


def _layer_weights(w_in, w_oa, w_ob, w_oc, w_out):
    col = lambda idx: w_in[:, IN_OFFSETS[idx]:IN_OFFSETS[idx + 1]]
    scale = HEAD_DIM ** -0.5
    ft = jnp.concatenate([col(4) * (scale * LOG2E), col(6), col(8) * (scale * LOG2E), col(10)], axis=1).T
    grp = lambda idx, gi: col(idx)[:, gi * A_WIDTH:(gi + 1) * A_WIDTH]
    a_qkv = lambda gi: jnp.concatenate([grp(0, gi) * (scale * LOG2E), grp(1, gi), grp(2, gi)],
                                       axis=1).astype(BF16)
    return {
        "a0": a_qkv(0), "a1": a_qkv(1), "a2": a_qkv(2),
        "kb": col(5).astype(BF16), "kc": col(9).astype(BF16), "ft": ft.astype(BF16),
        "g": jnp.concatenate([col(3), col(7), col(11), col(12)], axis=1).astype(BF16),
        "oa": w_oa.astype(BF16), "ob": w_ob.astype(BF16), "oc": w_oc.astype(BF16),
        "out": w_out.astype(BF16),
    }


def _trunk(x3, layers, final_g):
    batch, seq, _ = x3.shape
    x = x3.reshape(batch * seq, D_MODEL)
    fg = final_g.astype(F32).reshape(1, D_MODEL)
    for li, lw in enumerate(layers):
        w = lw["w"]
        a0, a1, a2, kb, kc, qbt, vbt, qct, vct = _proj_call(x, lw["norm_g"], w, batch, seq)
        oa, la = zip(*[_a_call(qkv, gi) for gi, qkv in enumerate((a0, a1, a2))])
        ob = _b_call(lw["sink"], qbt, kb, vbt, batch, seq)
        oc = _c_call(lw["scal"], lw["lamv"], lw["subln_g"], qct, kc, vct, batch, seq)
        x = _post_call(x, lw["norm_g"], oa, la, ob, oc, w, fg, final=(li == len(layers) - 1), seq=seq)
    return x.reshape(batch, seq, D_MODEL)


def _prepare_layers(norm_g, w_in, w_oa, w_ob, w_oc, w_out, b_sink, lam_q1, lam_k1, lam_q2, lam_k2, c_subln_g):
    c_slopes = jnp.asarray(_alibi_slopes(C_HEADS) * np.float32(LOG2E))
    layers = []
    for l in range(DEPTH):
        lam_init = 0.8 - 0.6 * math.exp(-0.3 * l)
        layers.append({
            "w": _layer_weights(w_in[l], w_oa[l], w_ob[l], w_oc[l], w_out[l]),
            "norm_g": norm_g[l].astype(F32).reshape(1, D_MODEL),
            "sink": b_sink[l].astype(F32) * LOG2E,
            "scal": jnp.concatenate([c_slopes, jnp.full((1,), lam_init, F32)]),
            "lamv": jnp.stack([lam_q1[l], lam_k1[l], lam_q2[l], lam_k2[l]]).astype(F32),
            "subln_g": c_subln_g[l],
        })
    return layers


def kernel(x_prompt, x_sample, norm_g, w_in, w_oa, w_ob, w_oc, w_out, b_sink, lam_q1, lam_k1, lam_q2, lam_k2, c_subln_g, final_norm_g):
    layers = _prepare_layers(norm_g, w_in, w_oa, w_ob, w_oc, w_out, b_sink,
                             lam_q1, lam_k1, lam_q2, lam_k2, c_subln_g)
    return (_trunk(x_prompt, layers, final_norm_g), _trunk(x_sample, layers, final_norm_g))
```

```python
import functools
import math

import numpy as np
import jax
import jax.numpy as jnp
from jax import lax
from jax.experimental import pallas as pl
from jax.experimental.pallas import tpu as pltpu

F32 = jnp.float32
BF16 = jnp.bfloat16

D_MODEL = 1024
DEPTH = 4
HEAD_DIM = 64
A_PATTERNS = ((128, 1), (512, 4), (2048, 16))
A_GROUPS = 3
A_HEADS = 8
A_WIDTH = A_HEADS * HEAD_DIM
A_HALF = 64
B_HEADS = 8
B_KV_HEADS = 2
B_HALF = 128
B_WIDTH = B_HEADS * HEAD_DIM
C_HEADS = 4
C_VDIM = 2 * HEAD_DIM
C_WIDTH = C_HEADS * C_VDIM
RMS_EPS = 1e-6
NEG_INF = -1e30
IN_SIZES = (
    A_GROUPS * A_WIDTH, A_GROUPS * A_WIDTH, A_GROUPS * A_WIDTH, A_WIDTH,
    B_WIDTH, B_KV_HEADS * HEAD_DIM, B_KV_HEADS * HEAD_DIM, B_WIDTH,
    2 * C_HEADS * HEAD_DIM, 2 * C_HEADS * HEAD_DIM, C_WIDTH, C_WIDTH,
    3 * D_MODEL,
)
IN_OFFSETS = tuple(int(c) for c in np.cumsum((0,) + IN_SIZES))

PROJ_TM = 512
POST_TM = 512
A_TQ = 128
A_STEP = 1024
B_TQ = 256
B_UNIT = 256
B_CHAINS = 2
C_TQ = 256
C_TK = 512
C_UNIT = 256
C_STAGES = 8
C_ONES_ROWS = 16
LOG2E = 1.4426950408889634
LANES = 128
VMEM_LIMIT = 56 * 2**20

N_FEAT = B_WIDTH + B_KV_HEADS * HEAD_DIM + 2 * C_HEADS * HEAD_DIM + C_WIDTH


def _alibi_slopes(n):
    return np.asarray([2.0 ** (-8.0 * (i + 1) / n) for i in range(n)], dtype=np.float32)


def _rms(x, g):
    ms = jnp.mean(x * x, axis=-1, keepdims=True)
    return x * lax.rsqrt(ms + RMS_EPS) * g


def _const_spec(shape):
    nd = len(shape)
    return pl.BlockSpec(shape, lambda *_: (0,) * nd, pipeline_mode=pl.Buffered(1))


def _proj_kernel(x_ref, g_ref, wa0_ref, wa1_ref, wa2_ref, wkb_ref, wkc_ref, wft_ref,
                 a0_ref, a1_ref, a2_ref, kb_ref, kc_ref, qbt_ref, vbt_ref, qct_ref, vct_ref, h_scr):
    tm = x_ref.shape[0]
    h32 = _rms(x_ref[...], g_ref[...])
    h = h32.astype(BF16)
    a0_ref[0, 0] = jnp.dot(h, wa0_ref[...], preferred_element_type=F32).astype(BF16)
    n_chunks = h_scr.shape[0]
    for c in range(n_chunks):
        h_scr[c] = h32[:, c * LANES:(c + 1) * LANES]
    for w_ref, o_ref, (_, dil) in ((wa1_ref, a1_ref, A_PATTERNS[1]), (wa2_ref, a2_ref, A_PATTERNS[2])):
        n = tm // dil
        hp = jnp.concatenate(
            [jnp.concatenate([h_scr[c, pl.ds(r, n, stride=dil), :] for c in range(n_chunks)], axis=1)
             for r in range(dil)], axis=0).astype(BF16)
        res = jnp.dot(hp, w_ref[...], preferred_element_type=F32).astype(BF16)
        for r in range(dil):
            o_ref[0, r] = res[r * n:(r + 1) * n]
    for w_ref, o_ref in ((wkb_ref, kb_ref), (wkc_ref, kc_ref)):
        o_ref[...] = jnp.dot(h, w_ref[...], preferred_element_type=F32).astype(BF16)
    ft = lax.dot_general(wft_ref[...], h, (((1,), (1,)), ((), ())),
                         preferred_element_type=F32).astype(BF16)
    r0 = 0
    for o_ref, rows in ((qbt_ref, B_WIDTH), (vbt_ref, B_KV_HEADS * HEAD_DIM),
                        (qct_ref, 2 * C_HEADS * HEAD_DIM), (vct_ref, C_WIDTH)):
        nblk, _, width = o_ref.shape
        for c in range(nblk):
            o_ref[c] = ft[r0:r0 + rows, c * width:(c + 1) * width]
        r0 += rows


def _proj_call(x, g, w, batch, seq):
    t = x.shape[0]
    tm = PROJ_TM
    nt = seq // tm
    tok = lambda n: pl.BlockSpec((tm, n), lambda i: (i, 0))
    feat = lambda rows, width: pl.BlockSpec((tm // width, rows, width), lambda i: (i, 0, 0))
    a_cols = 3 * A_WIDTH
    res = lambda dil: pl.BlockSpec((1, dil, tm // dil, a_cols), lambda i: (i // nt, 0, i % nt, 0))
    a_shape = lambda dil: jax.ShapeDtypeStruct((batch, dil, seq // dil, a_cols), BF16)
    dils = [d for _, d in A_PATTERNS]
    out_shape = (
        a_shape(dils[0]), a_shape(dils[1]), a_shape(dils[2]),
        jax.ShapeDtypeStruct((t, B_KV_HEADS * HEAD_DIM), BF16),
        jax.ShapeDtypeStruct((t, 2 * C_HEADS * HEAD_DIM), BF16),
        jax.ShapeDtypeStruct((t // B_TQ, B_WIDTH, B_TQ), BF16),
        jax.ShapeDtypeStruct((t // B_HALF, B_KV_HEADS * HEAD_DIM, B_HALF), BF16),
        jax.ShapeDtypeStruct((t // C_TQ, 2 * C_HEADS * HEAD_DIM, C_TQ), BF16),
        jax.ShapeDtypeStruct((t // C_TK, C_WIDTH, C_TK), BF16),
    )
    return pl.pallas_call(
        _proj_kernel,
        grid=(t // tm,),
        in_specs=[tok(D_MODEL), _const_spec((1, D_MODEL)),
                  _const_spec(w["a0"].shape), _const_spec(w["a1"].shape), _const_spec(w["a2"].shape),
                  _const_spec(w["kb"].shape), _const_spec(w["kc"].shape), _const_spec(w["ft"].shape)],
        out_specs=(res(dils[0]), res(dils[1]), res(dils[2]), tok(B_KV_HEADS * HEAD_DIM),
                   tok(2 * C_HEADS * HEAD_DIM),
                   feat(B_WIDTH, B_TQ), feat(B_KV_HEADS * HEAD_DIM, B_HALF),
                   feat(2 * C_HEADS * HEAD_DIM, C_TQ), feat(C_WIDTH, C_TK)),
        out_shape=out_shape,
        scratch_shapes=[pltpu.VMEM((D_MODEL // LANES, tm, LANES), F32)],
        compiler_params=pltpu.CompilerParams(dimension_semantics=("parallel",),
                                             vmem_limit_bytes=VMEM_LIMIT),
        name="proj",
    )(x, g, w["a0"], w["a1"], w["a2"], w["kb"], w["kc"], w["ft"])


def _a_bias(gi):
    dil = A_PATTERNS[gi][1]
    slopes = _alibi_slopes(A_GROUPS * A_HEADS).reshape(A_GROUPS, A_HEADS)[gi].astype(np.float64) * LOG2E
    cols = np.arange(A_TQ + 2 * A_HALF)[None, :]
    rel = (cols - A_HALF) - np.arange(A_TQ)[:, None]
    bias = -slopes[:, None, None] * (np.abs(rel) * float(dil))[None]
    out = []
    for variant in range(4):
        ok = np.abs(rel) <= A_HALF
        if variant & 1:
            ok = ok & (cols >= A_HALF)
        if variant & 2:
            ok = ok & (cols < A_TQ + A_HALF)
        out.append(np.where(ok[None], bias, NEG_INF))
    return np.stack(out).astype(np.float32)


def _a_kernel(cur_ref, prev_ref, next_ref, bias_first_ref, bias_mid_ref, bias_last_ref,
              o_ref, lse_ref, kfull, vfull, *, step, n_res):
    tkw = A_TQ + 2 * A_HALF
    for rr in range(n_res):
        for full, k0 in ((kfull, A_WIDTH), (vfull, 2 * A_WIDTH)):
            full[rr, 0:A_HALF] = prev_ref[0, rr, :, k0:k0 + A_WIDTH]
            full[rr, A_HALF:A_HALF + step] = cur_ref[0, rr, :, k0:k0 + A_WIDTH]
            full[rr, A_HALF + step:] = next_ref[0, rr, :, k0:k0 + A_WIDTH]
    n_sb = step // A_TQ
    lane = lax.broadcasted_iota(jnp.int32, (tkw, 2 * HEAD_DIM), 1)
    lane_o = lax.broadcasted_iota(jnp.int32, (A_TQ, 2 * HEAD_DIM), 1)
    ones_cols = jnp.ones((tkw, 2 * HEAD_DIM), BF16)
    tasks = [(rr, sb, p, hh) for rr in range(n_res) for sb in range(n_sb)
             for p in range(A_HEADS // 2) for hh in range(2)]

    def in_half(hh):
        return (lane >= hh * HEAD_DIM) & (lane < (hh + 1) * HEAD_DIM)

    def scores(task):
        rr, sb, p, hh = task
        r0, c0 = sb * A_TQ, p * 2 * HEAD_DIM
        bias_ref = bias_first_ref if sb == 0 else (bias_last_ref if sb == n_sb - 1 else bias_mid_ref)
        qp = cur_ref[0, rr, r0:r0 + A_TQ, c0:c0 + 2 * HEAD_DIM]
        kw = kfull[rr, r0:r0 + tkw, c0:c0 + 2 * HEAD_DIM]
        kz = jnp.where(in_half(hh), kw, jnp.zeros_like(kw))
        s = lax.dot_general(qp, kz, (((1,), (1,)), ((), ())), preferred_element_type=F32)
        s = s + bias_ref[0, 2 * p + hh]
        return s, jnp.max(s, axis=1, keepdims=True)

    def values(task, s, m):
        rr, sb, p, hh = task
        r0, c0 = sb * A_TQ, p * 2 * HEAD_DIM
        vw = vfull[rr, r0:r0 + tkw, c0:c0 + 2 * HEAD_DIM]
        vz = jnp.concatenate([jnp.where(in_half(hh), vw, jnp.zeros_like(vw)), ones_cols], axis=1)
        e = jnp.exp2(s - m)
        res = jnp.dot(e.astype(BF16), vz, preferred_element_type=F32)
        l = res[:, 2 * HEAD_DIM:]
        return res[:, :2 * HEAD_DIM] / l, m + jnp.log(l) * LOG2E

    o_pair = lse_pair = None
    for ti, task in enumerate(tasks):
        s, m = scores(task)
        o, lse = values(task, s, m)
        rr, sb, p, hh = task
        if hh == 0:
            o_pair, lse_pair = o, lse
        else:
            r0, c0 = sb * A_TQ, p * 2 * HEAD_DIM
            o_ref[0, rr, r0:r0 + A_TQ, c0:c0 + 2 * HEAD_DIM] = (o_pair + o).astype(o_ref.dtype)
            lse_ref[0, rr, r0:r0 + A_TQ, c0:c0 + 2 * HEAD_DIM] = jnp.where(lane_o < HEAD_DIM, lse_pair, lse)


def _a_call(qkv, gi):
    batch, dil, sub, cols = qkv.shape
    step = min(A_STEP, sub)
    n_res = min(dil, A_STEP // step)
    n_steps = sub // step
    hb = step // A_HALF
    nhb = sub // A_HALF
    cur = lambda width: pl.BlockSpec((1, n_res, step, width), lambda b, r, u: (b, r, u, 0))
    prev = pl.BlockSpec((1, n_res, A_HALF, cols), lambda b, r, u: (b, r, jnp.maximum(u * hb - 1, 0), 0))
    nxt = pl.BlockSpec((1, n_res, A_HALF, cols),
                       lambda b, r, u: (b, r, jnp.minimum((u + 1) * hb, nhb - 1), 0))
    bias = jnp.asarray(_a_bias(gi))
    at_start = lambda u: (u == 0).astype(jnp.int32)
    at_end = lambda u: 2 * (u == n_steps - 1).astype(jnp.int32)
    single = step == A_TQ
    bias_spec = lambda variant: pl.BlockSpec((1,) + bias.shape[1:], lambda b, r, u: (variant(u), 0, 0, 0))
    return pl.pallas_call(
        functools.partial(_a_kernel, step=step, n_res=n_res),
        grid=(batch, dil // n_res, n_steps),
        in_specs=[cur(cols), prev, nxt,
                  bias_spec(lambda u: at_start(u) + (at_end(u) if single else 0)),
                  bias_spec(lambda u: 0 * u), bias_spec(at_end)],
        out_specs=(cur(A_WIDTH), cur(A_WIDTH)),
        out_shape=(jax.ShapeDtypeStruct((batch, dil, sub, A_WIDTH), BF16),
                   jax.ShapeDtypeStruct((batch, dil, sub, A_WIDTH), F32)),
        scratch_shapes=[pltpu.VMEM((n_res, step + 2 * A_HALF, A_WIDTH), BF16),
                        pltpu.VMEM((n_res, step + 2 * A_HALF, A_WIDTH), BF16)],
        compiler_params=pltpu.CompilerParams(
            dimension_semantics=("parallel", "parallel", "parallel"), vmem_limit_bytes=VMEM_LIMIT),
        name=f"mixer_a{gi}",
    )(qkv, qkv, qkv, bias, bias, bias)


def _b_bias():
    slopes = _alibi_slopes(B_HEADS).astype(np.float64) * LOG2E
    rows = np.arange(B_TQ + 2 * B_HALF)[:, None]
    rel = (rows - B_HALF) - np.arange(B_TQ)[None, :]
    bias = -slopes[:, None, None] * np.abs(rel)[None]
    out = []
    for variant in range(4):
        ok = np.abs(rel) <= B_HALF
        if variant & 1:
            ok = ok & (rows >= B_HALF)
        if variant & 2:
            ok = ok & (rows < B_HALF + B_TQ)
        out.append(np.where(ok[None], bias, NEG_INF))
    return np.stack(out).astype(np.float32)


def _b_kernel(sink_ref, q_ref, kp_ref, kc_ref, kn_ref, vp_ref, vc_ref, vn_ref, bias_ref,
              o_ref, kfull, vfull, s_scr):
    tk = B_TQ + 2 * B_HALF
    n_units = tk // B_UNIT
    kfull[0:B_HALF] = kp_ref[...]
    kfull[B_HALF:B_HALF + B_TQ] = kc_ref[...]
    kfull[B_HALF + B_TQ:] = kn_ref[...]
    vfull[:, 0:B_HALF] = vp_ref[0]
    for c in range(B_TQ // B_HALF):
        vfull[:, (c + 1) * B_HALF:(c + 2) * B_HALF] = vc_ref[c]
    vfull[:, B_HALF + B_TQ:] = vn_ref[0]
    zeros = jnp.zeros((HEAD_DIM, B_TQ), BF16)
    ones_rows = (lax.broadcasted_iota(jnp.int32, (C_ONES_ROWS, B_UNIT), 0) == 0).astype(BF16)
    neg = jnp.full((1, B_TQ), NEG_INF, F32)
    grp = B_HEADS // B_KV_HEADS

    def score_unit(h, u, s_scr, bm):
        r0 = u * B_UNIT
        qh = q_ref[0, h * HEAD_DIM:(h + 1) * HEAD_DIM, :]
        rhs = jnp.concatenate([qh, zeros] if h // grp == 0 else [zeros, qh], axis=0)
        s = (jnp.dot(kfull[r0:r0 + B_UNIT, :], rhs, preferred_element_type=F32)
             + bias_ref[0, h, r0:r0 + B_UNIT, :])
        s_scr[r0:r0 + B_UNIT, :] = s
        return jnp.maximum(bm, jnp.max(s, axis=0, keepdims=True))

    def value_unit(h, u, s_scr, m, pv):
        r0 = u * B_UNIT
        kvh = h // grp
        v = jnp.concatenate([vfull[kvh * HEAD_DIM:(kvh + 1) * HEAD_DIM, r0:r0 + B_UNIT], ones_rows],
                            axis=0)
        e = jnp.exp2(s_scr[r0:r0 + B_UNIT, :] - m)
        out = jnp.dot(v, e.astype(BF16), preferred_element_type=F32)
        return out if pv is None else pv + out

    per_chain = B_HEADS // B_CHAINS
    scr = lambda c, t: s_scr.at[c, t % 2]
    bms = [neg] * B_CHAINS
    for u in range(n_units):
        bms = [score_unit(c, u, scr(c, 0), bms[c]) for c in range(B_CHAINS)]
    outs = [None] * B_HEADS
    for t in range(per_chain):
        heads = [c + t * B_CHAINS for c in range(B_CHAINS)]
        sinks = [sink_ref[h] for h in heads]
        ms = [jnp.maximum(bms[c], sinks[c]) for c in range(B_CHAINS)]
        bms = [neg] * B_CHAINS
        pvs = [None] * B_CHAINS
        for u in range(n_units):
            if t + 1 < per_chain:
                bms = [score_unit(heads[c] + B_CHAINS, u, scr(c, t + 1), bms[c]) for c in range(B_CHAINS)]
            pvs = [value_unit(heads[c], u, scr(c, t), ms[c], pvs[c]) for c in range(B_CHAINS)]
        for c, h in enumerate(heads):
            l = pvs[c][HEAD_DIM:HEAD_DIM + 1] + jnp.exp2(sinks[c] - ms[c])
            outs[h] = pvs[c][:HEAD_DIM] / l
    o_ref[...] = jnp.concatenate(outs, axis=0).T.astype(o_ref.dtype)


def _b_call(sink2, qbt, kb, vbt, batch, seq):
    t = batch * seq
    n_q = seq // B_TQ
    nkb = seq // B_HALF
    r = B_TQ // B_HALF
    kvw = B_KV_HEADS * HEAD_DIM
    tk = B_TQ + 2 * B_HALF
    prev_i = lambda b, i: b * nkb + jnp.maximum(i * r - 1, 0)
    next_i = lambda b, i: b * nkb + jnp.minimum((i + 1) * r, nkb - 1)
    variant = lambda b, i: ((i == 0).astype(jnp.int32) + 2 * (i == n_q - 1).astype(jnp.int32), 0, 0, 0)
    bias = jnp.asarray(_b_bias())
    return pl.pallas_call(
        _b_kernel,
        grid=(batch, n_q),
        in_specs=[pl.BlockSpec(memory_space=pltpu.SMEM),
                  pl.BlockSpec((1, B_WIDTH, B_TQ), lambda b, i: (b * n_q + i, 0, 0)),
                  pl.BlockSpec((B_HALF, kvw), lambda b, i: (prev_i(b, i), 0)),
                  pl.BlockSpec((B_TQ, kvw), lambda b, i: (b * n_q + i, 0)),
                  pl.BlockSpec((B_HALF, kvw), lambda b, i: (next_i(b, i), 0)),
                  pl.BlockSpec((1, kvw, B_HALF), lambda b, i: (prev_i(b, i), 0, 0)),
                  pl.BlockSpec((r, kvw, B_HALF), lambda b, i: (b * n_q + i, 0, 0)),
                  pl.BlockSpec((1, kvw, B_HALF), lambda b, i: (next_i(b, i), 0, 0)),
                  pl.BlockSpec((1,) + bias.shape[1:], variant)],
        out_specs=pl.BlockSpec((B_TQ, B_WIDTH), lambda b, i: (b * n_q + i, 0)),
        out_shape=jax.ShapeDtypeStruct((t, B_WIDTH), BF16),
        scratch_shapes=[pltpu.VMEM((tk, kvw), BF16), pltpu.VMEM((kvw, tk), BF16),
                        pltpu.VMEM((B_CHAINS, 2, tk, B_TQ), F32)],
        compiler_params=pltpu.CompilerParams(dimension_semantics=("parallel", "parallel"),
                                             vmem_limit_bytes=VMEM_LIMIT),
        name="mixer_b",
    )(sink2, qbt, kb, kb, kb, vbt, vbt, vbt, bias)


def _c_bias():
    slopes = _alibi_slopes(C_HEADS).astype(np.float64) * LOG2E
    d = (np.arange(C_TK)[:, None] - np.arange(C_TQ)[None, :]).astype(np.float64)
    tiles = [d, -d]
    for part in range(C_TK // C_TQ):
        tiles.append(-np.abs(d - part * C_TQ))
    return (slopes[:, None, None, None] * np.stack(tiles)[None]).astype(np.float32)


def _c_kernel(scal_ref, lamv_ref, q_ref, k_ref, v_ref, bias_ref, g_ref, o_ref,
              s_even, s_odd, acc, *, n_q, n_kv, stages):
    h = pl.program_id(1)
    ratio = C_TK // C_TQ
    n_units = C_TK // C_UNIT
    slope2 = scal_ref[h]
    lam_init = scal_ref[C_HEADS]
    zeros = jnp.zeros((HEAD_DIM, C_TQ), BF16)
    ones_rows = (lax.broadcasted_iota(jnp.int32, (C_ONES_ROWS, C_UNIT), 0) == 0).astype(BF16)
    neg = jnp.full((1, C_TQ), NEG_INF, F32)
    lv = lamv_ref[...]
    lam = (jnp.exp(jnp.sum(lv[0:1] * lv[1:2], axis=1, keepdims=True))
           - jnp.exp(jnp.sum(lv[2:3] * lv[3:4], axis=1, keepdims=True)) + lam_init)
    acc[...] = jnp.ones_like(acc)

    def block_offset(i, j):
        return jnp.where(j == i // ratio, 0.0, -slope2 * jnp.abs(C_TQ * i - C_TK * j).astype(F32))

    def score_units(i, j, s_scr):
        jd = i // ratio
        tile = jnp.where(j < jd, 0, jnp.where(j > jd, 1, 2 + i % ratio))
        q = q_ref[i]
        rhs = (jnp.concatenate([q[:HEAD_DIM], zeros], axis=0),
               jnp.concatenate([zeros, q[HEAD_DIM:]], axis=0))
        bm = [neg, neg]
        for u in range(n_units):
            r0 = u * C_UNIT
            k = k_ref[pl.ds(pl.multiple_of(j * C_TK + r0, C_UNIT), C_UNIT), :]
            bias = bias_ref[0, tile, r0:r0 + C_UNIT, :]
            for mp in range(2):
                s = jnp.dot(k, rhs[mp], preferred_element_type=F32) + bias
                s_scr[mp, r0:r0 + C_UNIT, :] = s
                bm[mp] = jnp.maximum(bm[mp], jnp.max(s, axis=0, keepdims=True))
            yield tuple(bm)

    def value_unit(j, u, s_scr, mn, pvs):
        r0 = u * C_UNIT
        v = jnp.concatenate([v_ref[j, :, r0:r0 + C_UNIT], ones_rows], axis=0)
        out = []
        for mp in range(2):
            e = jnp.exp2(s_scr[mp, r0:r0 + C_UNIT, :] - mn[mp])
            pv = jnp.dot(v, e.astype(BF16), preferred_element_type=F32)
            out.append(pv if pvs[mp] is None else pvs[mp] + pv)
        return tuple(out)

    def stage(aset, i, j, s_cur, bm_cur, m, i_next, j_next, s_next):
        c = block_offset(i, j)
        mn, alpha, m_new = [], [], []
        for mp in range(2):
            msh = m[mp] - c
            x = jnp.maximum(msh, bm_cur[mp])
            mn.append(x)
            alpha.append(jnp.exp2(msh - x))
            m_new.append(x + c)
        bm_next = (neg, neg)
        pvs = (None, None)
        nxt = score_units(i_next, j_next, s_next)
        for u in range(n_units):
            bm_next = next(nxt)
            pvs = value_unit(j, u, s_cur, mn, pvs)
        for mp in range(2):
            acc[aset, mp] = alpha[mp] * acc[aset, mp] + pvs[mp]
        return bm_next, tuple(m_new)

    def run_stages(aset, i, j0, count, bm, m, last_next, after_first=None):
        for d in range(count):
            j = j0 + d
            cur, oth = (s_even, s_odd) if d % 2 == 0 else (s_odd, s_even)
            i_n, j_n = (i, j + 1) if d + 1 < count or last_next is None else last_next
            bm, m = stage(aset, i, j, cur, bm, m, i_n, j_n, oth)
            if d == 0 and after_first is not None:
                after_first()
        return bm, m

    def finalize(aset, i):
        a1 = acc[aset, 0]
        a2 = acc[aset, 1]
        a = a1[:C_VDIM] / a1[C_VDIM:C_VDIM + 1] - lam * (a2[:C_VDIM] / a2[C_VDIM:C_VDIM + 1])
        ms = jnp.mean(a * a, axis=0, keepdims=True)
        y = a * lax.rsqrt(ms + RMS_EPS) * g_ref[...] * (1.0 - lam_init)
        o_ref[pl.ds(pl.multiple_of(i * C_TQ, C_TQ), C_TQ), :] = y.T.astype(o_ref.dtype)

    def q_block(aset, i, bm, fin_i):
        m = (neg, neg)
        after_first = lambda: finalize(1 - aset, fin_i)
        trips = n_kv // stages - 1
        if trips > 0:
            bm, m = run_stages(aset, i, 0, stages, bm, m, None, after_first)
            after_first = None
            if trips > 1:
                def trip(t, carry):
                    return run_stages(aset, i, t * stages, stages, carry[0], carry[1], None)
                bm, m = lax.fori_loop(1, trips, trip, (bm, m))
        i_next = jnp.minimum(i + 1, n_q - 1)
        bm, _ = run_stages(aset, i, n_kv - stages, stages, bm, m, (i_next, 0), after_first)
        return bm

    def q_pair(ip, bm):
        i0 = 2 * ip
        bm = q_block(0, i0, bm, jnp.maximum(i0 - 1, 0))
        return q_block(1, i0 + 1, bm, i0)

    bm0 = (neg, neg)
    for bm0 in score_units(0, 0, s_even):
        pass
    lax.fori_loop(0, n_q // 2, q_pair, bm0)
    finalize(1, n_q - 1)


def _c_call(scal, lamv, subln_g, qct, kc, vct, batch, seq):
    t = batch * seq
    n_q = seq // C_TQ
    n_kv = seq // C_TK
    kw = 2 * HEAD_DIM
    stages = min(C_STAGES, n_kv)
    assert n_kv % stages == 0 and stages % 2 == 0 and n_q % 2 == 0
    bias = jnp.asarray(_c_bias())
    g = jnp.broadcast_to(subln_g.astype(F32)[:, None], (C_VDIM, C_TQ))
    return pl.pallas_call(
        functools.partial(_c_kernel, n_q=n_q, n_kv=n_kv, stages=stages),
        grid=(batch, C_HEADS),
        in_specs=[pl.BlockSpec(memory_space=pltpu.SMEM),
                  _const_spec((4, HEAD_DIM)),
                  pl.BlockSpec((n_q, kw, C_TQ), lambda b, h: (b, h, 0)),
                  pl.BlockSpec((seq, kw), lambda b, h: (b, h)),
                  pl.BlockSpec((n_kv, C_VDIM, C_TK), lambda b, h: (b, h, 0)),
                  pl.BlockSpec((1,) + bias.shape[1:], lambda b, h: (h, 0, 0, 0)),
                  _const_spec((C_VDIM, C_TQ))],
        out_specs=pl.BlockSpec((seq, C_VDIM), lambda b, h: (b, h)),
        out_shape=jax.ShapeDtypeStruct((t, C_WIDTH), BF16),
        scratch_shapes=[pltpu.VMEM((2, C_TK, C_TQ), F32), pltpu.VMEM((2, C_TK, C_TQ), F32),
                        pltpu.VMEM((2, 2, C_VDIM + C_ONES_ROWS, C_TQ), F32)],
        compiler_params=pltpu.CompilerParams(
            dimension_semantics=("parallel", "parallel"), vmem_limit_bytes=VMEM_LIMIT),
        name="mixer_c",
    )(scal, lamv, qct, kc, vct, bias, g)


def _sigmoid(x):
    return 1.0 / (1.0 + jnp.exp(-x))


def _post_kernel(x_ref, g_ref, oa0_ref, oa1_ref, oa2_ref, la0_ref, la1_ref, la2_ref, ob_ref, oc_ref,
                 wg_ref, woa_ref, wob_ref, woc_ref, wout_ref, fg_ref, y_ref, tok_scr, *, final):
    x = x_ref[...]
    h = _rms(x, g_ref[...]).astype(BF16)

    def token_order(ref, slot):
        _, dil, n, width = ref.shape
        if dil == 1:
            return ref[0, 0]
        n_chunks = width // LANES
        for r in range(dil):
            rows = ref[0, r].astype(F32)
            for c in range(n_chunks):
                tok_scr[slot, c, pl.ds(r, n, stride=dil), :] = rows[:, c * LANES:(c + 1) * LANES]
        return jnp.concatenate([tok_scr[slot, c] for c in range(n_chunks)], axis=1)

    def gate(c0, n):
        return jnp.dot(h, wg_ref[:, c0:c0 + n], preferred_element_type=F32)

    def silu_gated(o, ga):
        return (o * (ga * _sigmoid(ga))).astype(BF16)

    g0 = A_WIDTH + B_WIDTH + C_WIDTH
    ga, gb, gc = gate(0, A_WIDTH), gate(A_WIDTH, B_WIDTH), gate(A_WIDTH + B_WIDTH, C_WIDTH)
    merge = [_sigmoid(gate(g0 + b * D_MODEL, D_MODEL)) for b in range(3)]

    l0, l1, l2 = token_order(la0_ref, 0), token_order(la1_ref, 0), token_order(la2_ref, 1)
    o0, o1, o2 = token_order(oa0_ref, 0), token_order(oa1_ref, 2), token_order(oa2_ref, 3)
    m = jnp.maximum(jnp.maximum(l0, l1), l2)
    e0, e1, e2 = jnp.exp2(l0 - m), jnp.exp2(l1 - m), jnp.exp2(l2 - m)
    oa = (e0 * o0 + e1 * o1 + e2 * o2) / (e0 + e1 + e2)

    ub, uc, ua = silu_gated(ob_ref[...], gb), silu_gated(oc_ref[...], gc), silu_gated(oa, ga)
    yb = jnp.dot(ub, wob_ref[...], preferred_element_type=F32)
    yc = jnp.dot(uc, woc_ref[...], preferred_element_type=F32)
    ya = jnp.dot(ua, woa_ref[...], preferred_element_type=F32)
    mixed = merge[0] * ya + merge[1] * yb + merge[2] * yc
    y = x + jnp.dot(mixed.astype(BF16), wout_ref[...], preferred_element_type=F32)
    if final:
        y = _rms(y, fg_ref[...])
    y_ref[...] = y


def _post_call(x, g, oa, la, ob, oc, w, final_g, final, seq):
    t = x.shape[0]
    tm = POST_TM
    nt = seq // tm
    tok = lambda n: pl.BlockSpec((tm, n), lambda i: (i, 0))
    res = [pl.BlockSpec((1, dil, tm // dil, A_WIDTH), lambda i: (i // nt, 0, i % nt, 0))
           for _, dil in A_PATTERNS]
    return pl.pallas_call(
        functools.partial(_post_kernel, final=final),
        grid=(t // tm,),
        in_specs=[tok(D_MODEL), _const_spec((1, D_MODEL))] + res + res + [tok(B_WIDTH), tok(C_WIDTH)]
                 + [_const_spec(w["g"].shape), _const_spec(w["oa"].shape), _const_spec(w["ob"].shape),
                    _const_spec(w["oc"].shape), _const_spec(w["out"].shape), _const_spec((1, D_MODEL))],
        out_specs=tok(D_MODEL),
        out_shape=jax.ShapeDtypeStruct((t, D_MODEL), F32),
        scratch_shapes=[pltpu.VMEM((4, A_WIDTH // LANES, tm, LANES), F32)],
        compiler_params=pltpu.CompilerParams(dimension_semantics=("parallel",),
                                             vmem_limit_bytes=VMEM_LIMIT),
        name="post_final" if final else "post",
    )(x, g, oa[0], oa[1], oa[2], la[0], la[1], la[2], ob, oc,
      w["g"], w["oa"], w["ob"], w["oc"], w["out"], final_g)


def _layer_weights(w_in, w_oa, w_ob, w_oc, w_out):
    col = lambda idx: w_in[:, IN_OFFSETS[idx]:IN_OFFSETS[idx + 1]]
    scale = HEAD_DIM ** -0.5
    ft = jnp.concatenate([col(4) * (scale * LOG2E), col(6), col(8) * (scale * LOG2E), col(10)], axis=1).T
    grp = lambda idx, gi: col(idx)[:, gi * A_WIDTH:(gi + 1) * A_WIDTH]
    a_qkv = lambda gi: jnp.concatenate([grp(0, gi) * (scale * LOG2E), grp(1, gi), grp(2, gi)],
                                       axis=1).astype(BF16)
    return {
        "a0": a_qkv(0), "a1": a_qkv(1), "a2": a_qkv(2),
        "kb": col(5).astype(BF16), "kc": col(9).astype(BF16), "ft": ft.astype(BF16),
        "g": jnp.concatenate([col(3), col(7), col(11), col(12)], axis=1).astype(BF16),
        "oa": w_oa.astype(BF16), "ob": w_ob.astype(BF16), "oc": w_oc.astype(BF16),
        "out": w_out.astype(BF16),
    }


def _trunk(x3, layers, final_g):
    batch, seq, _ = x3.shape
    x = x3.reshape(batch * seq, D_MODEL)
    fg = final_g.astype(F32).reshape(1, D_MODEL)
    for li, lw in enumerate(layers):
        w = lw["w"]
        a0, a1, a2, kb, kc, qbt, vbt, qct, vct = _proj_call(x, lw["norm_g"], w, batch, seq)
        oa, la = zip(*[_a_call(qkv, gi) for gi, qkv in enumerate((a0, a1, a2))])
        ob = _b_call(lw["sink"], qbt, kb, vbt, batch, seq)
        oc = _c_call(lw["scal"], lw["lamv"], lw["subln_g"], qct, kc, vct, batch, seq)
        x = _post_call(x, lw["norm_g"], oa, la, ob, oc, w, fg, final=(li == len(layers) - 1), seq=seq)
    return x.reshape(batch, seq, D_MODEL)


def _prepare_layers(norm_g, w_in, w_oa, w_ob, w_oc, w_out, b_sink, lam_q1, lam_k1, lam_q2, lam_k2, c_subln_g):
    c_slopes = jnp.asarray(_alibi_slopes(C_HEADS) * np.float32(LOG2E))
    layers = []
    for l in range(DEPTH):
        lam_init = 0.8 - 0.6 * math.exp(-0.3 * l)
        layers.append({
            "w": _layer_weights(w_in[l], w_oa[l], w_ob[l], w_oc[l], w_out[l]),
            "norm_g": norm_g[l].astype(F32).reshape(1, D_MODEL),
            "sink": b_sink[l].astype(F32) * LOG2E,
            "scal": jnp.concatenate([c_slopes, jnp.full((1,), lam_init, F32)]),
            "lamv": jnp.stack([lam_q1[l], lam_k1[l], lam_q2[l], lam_k2[l]]).astype(F32),
            "subln_g": c_subln_g[l],
        })
    return layers


def kernel(x_prompt, x_sample, norm_g, w_in, w_oa, w_ob, w_oc, w_out, b_sink, lam_q1, lam_k1, lam_q2, lam_k2, c_subln_g, final_norm_g):
    layers = _prepare_layers(norm_g, w_in, w_oa, w_ob, w_oc, w_out, b_sink,
                             lam_q1, lam_k1, lam_q2, lam_k2, c_subln_g)
    return (_trunk(x_prompt, layers, final_norm_g), _trunk(x_sample, layers, final_norm_g))
```
